```python
import jax, jax.numpy as jnp
from jax import lax
import numpy as np

D_MODEL = 1024
BATCH = 16
SEQ = 2048
DEPTH = 1

GRID_W = 64
HEAD_DIM = 64
NA_HEADS = 8
NA_WIN_H = 8
NA_WIN_W = 16
GQA_HEADS = 8
GQA_KV_HEADS = 2
Q_BLOCK = 128
ROPE_THETA = 10000.0
N_EXPERTS = 16
EC_CAPACITY_FACTOR = 2
D_EXPERT = 2816
NORM_EPS = 1e-6

NA_WIDTH = NA_HEADS * HEAD_DIM
GQA_WIDTH = GQA_HEADS * HEAD_DIM
GQA_KV_WIDTH = GQA_KV_HEADS * HEAD_DIM
MIX_WIDTH = NA_WIDTH + GQA_WIDTH
IN_WIDTH = 3 * NA_WIDTH + GQA_WIDTH + 2 * GQA_KV_WIDTH

kernel_name = "hymba_natten_gqa_axialrope_expert_choice"


def rms_norm(x, g):
    xf = x.astype(jnp.float32)
    y = xf * lax.rsqrt(jnp.mean(xf * xf, axis=-1, keepdims=True) + NORM_EPS)
    return (y * g.astype(jnp.float32)).astype(x.dtype)


def neighbourhood_attention(q, k, v, rpb):
    B, S, H, D = q.shape
    rows = S // GRID_W
    kh = min(NA_WIN_H, rows)
    kw = NA_WIN_W
    n_nb = kh * kw
    qg = q.reshape(B, rows, GRID_W, H, D)
    kg = k.reshape(B, rows, GRID_W, H, D)
    vg = v.reshape(B, rows, GRID_W, H, D)
    row_start = jnp.clip(jnp.arange(rows) - kh // 2, 0, rows - kh)
    col = jnp.arange(GRID_W)
    col_start = jnp.clip(col - kw // 2, 0, GRID_W - kw)
    col_idx = col_start[:, None] + jnp.arange(kw)[None, :]
    col_off = col_idx - col[:, None] + (NA_WIN_W - 1)
    scale = D ** -0.5

    def one_row(r):
        rs = row_start[r]
        k_rows = lax.dynamic_slice_in_dim(kg, rs, kh, axis=1)
        v_rows = lax.dynamic_slice_in_dim(vg, rs, kh, axis=1)
        k_nb = k_rows[:, :, col_idx].transpose(0, 2, 1, 3, 4, 5).reshape(B, GRID_W, n_nb, H, D)
        v_nb = v_rows[:, :, col_idx].transpose(0, 2, 1, 3, 4, 5).reshape(B, GRID_W, n_nb, H, D)
        q_r = lax.dynamic_index_in_dim(qg, r, axis=1, keepdims=False)
        s = jnp.einsum('bwhd,bwnhd->bhwn', q_r, k_nb).astype(jnp.float32) * scale
        row_off = rs + jnp.arange(kh) - r + (NA_WIN_H - 1)
        bias = rpb[:, row_off[:, None, None], col_off[None, :, :]]
        bias = bias.transpose(0, 2, 1, 3).reshape(H, GRID_W, n_nb)
        p = jax.nn.softmax(s + bias[None].astype(jnp.float32), axis=-1).astype(v.dtype)
        return jnp.einsum('bhwn,bwnhd->bwhd', p, v_nb)

    o = lax.map(one_row, jnp.arange(rows))
    return o.transpose(1, 0, 2, 3, 4).reshape(B, S, H * D)


def axial_rope_tables(S):
    t = jnp.arange(S)
    row = (t // GRID_W).astype(jnp.float32)
    col = (t % GRID_W).astype(jnp.float32)
    half = HEAD_DIM // 2
    inv_freq = 1.0 / (ROPE_THETA ** (jnp.arange(0, half, 2, dtype=jnp.float32) / half))
    ang = jnp.concatenate([row[:, None] * inv_freq[None], col[:, None] * inv_freq[None]], axis=-1)
    return jnp.cos(ang), jnp.sin(ang)


def apply_rope(x, cos, sin):
    B, S, H, D = x.shape
    xf = x.astype(jnp.float32).reshape(B, S, H, D // 2, 2)
    x0, x1 = xf[..., 0], xf[..., 1]
    c = cos[None, :, None, :]
    s = sin[None, :, None, :]
    out = jnp.stack([x0 * c - x1 * s, x0 * s + x1 * c], axis=-1)
    return out.reshape(B, S, H, D).astype(x.dtype)


def gqa_attention(q, k, v):
    B, S, Hq, D = q.shape
    Hkv = k.shape[2]
    G = Hq // Hkv
    scale = D ** -0.5
    qb = q.reshape(B, S // Q_BLOCK, Q_BLOCK, Hkv, G, D).transpose(1, 0, 2, 3, 4, 5)

    def block(qi):
        s = jnp.einsum('bqkgd,bskd->bkgqs', qi, k).astype(jnp.float32) * scale
        p = jax.nn.softmax(s, axis=-1).astype(v.dtype)
        return jnp.einsum('bkgqs,bskd->bqkgd', p, v)

    o = lax.map(block, qb)
    return o.transpose(1, 0, 2, 3, 4, 5).reshape(B, S, Hq * D)


def expert_choice_ffn(h, w_router, w_gate, w_up, w_down):
    B, S, D = h.shape
    cap = EC_CAPACITY_FACTOR * S // N_EXPERTS
    logits = jnp.einsum('bsd,de->bse', h, w_router).astype(jnp.float32)
    affinity = jax.nn.softmax(logits, axis=-1)
    gate, idx = lax.top_k(affinity.transpose(0, 2, 1), cap)
    xin = jax.vmap(lambda hb, ib: hb[ib])(h, idx)
    g = jnp.einsum('becd,edf->becf', xin, w_gate)
    u = jnp.einsum('becd,edf->becf', xin, w_up)
    y = jnp.einsum('becf,efd->becd', jax.nn.silu(g) * u, w_down)
    y = y * gate[..., None].astype(y.dtype)
    return jax.vmap(
        lambda ib, yb: jnp.zeros((S, D), yb.dtype).at[ib.reshape(-1)].add(yb.reshape(-1, D))
    )(idx, y)


def setup_inputs(seed: int = 0) -> dict:
    key = jax.random.key(seed)
    ks = jax.random.split(key, 16)
    f32 = jnp.float32

    def nrm(k, shape, fan_in):
        return jax.random.normal(k, shape, f32) * (fan_in ** -0.5)

    def gain(k, shape):
        return 1.0 + 0.02 * jax.random.normal(k, shape, f32)

    return {
        "x": jax.random.normal(ks[0], (BATCH, SEQ, D_MODEL), f32),
        "norm1_g": gain(ks[1], (DEPTH, D_MODEL)),
        "w_in": nrm(ks[2], (DEPTH, D_MODEL, IN_WIDTH), D_MODEL),
        "na_q_norm_g": gain(ks[3], (DEPTH, HEAD_DIM)),
        "na_k_norm_g": gain(ks[4], (DEPTH, HEAD_DIM)),
        "na_rpb": 0.02 * jax.random.normal(ks[5], (DEPTH, NA_HEADS, 2 * NA_WIN_H - 1, 2 * NA_WIN_W - 1), f32),
        "gqa_q_norm_g": gain(ks[6], (DEPTH, HEAD_DIM)),
        "gqa_k_norm_g": gain(ks[7], (DEPTH, HEAD_DIM)),
        "na_out_g": gain(ks[8], (DEPTH, NA_WIDTH)),
        "gqa_out_g": gain(ks[9], (DEPTH, GQA_WIDTH)),
        "w_out": nrm(ks[10], (DEPTH, MIX_WIDTH, D_MODEL), MIX_WIDTH),
        "norm2_g": gain(ks[11], (DEPTH, D_MODEL)),
        "w_router": nrm(ks[12], (DEPTH, D_MODEL, N_EXPERTS), D_MODEL),
        "w_gate": nrm(ks[13], (DEPTH, N_EXPERTS, D_MODEL, D_EXPERT), D_MODEL),
        "w_up": nrm(ks[14], (DEPTH, N_EXPERTS, D_MODEL, D_EXPERT), D_MODEL),
        "w_down": nrm(ks[15], (DEPTH, N_EXPERTS, D_EXPERT, D_MODEL), D_EXPERT),
    }


def reference(x, norm1_g, w_in, na_q_norm_g, na_k_norm_g, na_rpb, gqa_q_norm_g, gqa_k_norm_g,
              na_out_g, gqa_out_g, w_out, norm2_g, w_router, w_gate, w_up, w_down):
    B, S, _ = x.shape
    cos, sin = axial_rope_tables(S)
    splits = np.cumsum([NA_WIDTH, NA_WIDTH, NA_WIDTH, GQA_WIDTH, GQA_KV_WIDTH]).tolist()
    for l in range(DEPTH):
        u = rms_norm(x, norm1_g[l])
        proj = jnp.einsum('bsd,dn->bsn', u, w_in[l])
        qa, ka, va, qb, kb, vb = jnp.split(proj, splits, axis=-1)
        qa = rms_norm(qa.reshape(B, S, NA_HEADS, HEAD_DIM), na_q_norm_g[l])
        ka = rms_norm(ka.reshape(B, S, NA_HEADS, HEAD_DIM), na_k_norm_g[l])
        va = va.reshape(B, S, NA_HEADS, HEAD_DIM)
        out_a = neighbourhood_attention(qa, ka, va, na_rpb[l])
        qb = rms_norm(qb.reshape(B, S, GQA_HEADS, HEAD_DIM), gqa_q_norm_g[l])
        kb = rms_norm(kb.reshape(B, S, GQA_KV_HEADS, HEAD_DIM), gqa_k_norm_g[l])
        vb = vb.reshape(B, S, GQA_KV_HEADS, HEAD_DIM)
        qb = apply_rope(qb, cos, sin)
        kb = apply_rope(kb, cos, sin)
        out_b = gqa_attention(qb, kb, vb)
        mixed = jnp.concatenate([rms_norm(out_a, na_out_g[l]), rms_norm(out_b, gqa_out_g[l])], axis=-1)
        x = x + jnp.einsum('bsm,md->bsd', mixed, w_out[l])
        hn = rms_norm(x, norm2_g[l])
        x = x + expert_choice_ffn(hn, w_router[l], w_gate[l], w_up[l], w_down[l])
    return x
```

```python
import functools

import jax
import jax.numpy as jnp
import numpy as np
from jax import lax
from jax.experimental import pallas as pl
from jax.experimental.pallas import tpu as pltpu

F32 = jnp.float32
BF16 = jnp.bfloat16

GRID_W = 64
HEAD_DIM = 64
NA_HEADS = 8
NA_WIN_H = 8
NA_WIN_W = 16
GQA_HEADS = 8
GQA_KV_HEADS = 2
ROPE_THETA = 10000.0
N_EXPERTS = 16
EC_CAPACITY_FACTOR = 2
NORM_EPS = 1e-6

LANES = 128
NA_WIDTH = NA_HEADS * HEAD_DIM
GQA_WIDTH = GQA_HEADS * HEAD_DIM
GQA_KV_WIDTH = GQA_KV_HEADS * HEAD_DIM
NA_QROWS = 2
NA_KROWS = NA_WIN_H + NA_QROWS
MASK_NEG = -1e30

VMEM_LIMIT = 56 * 1024 * 1024


def _cparams(sem):
    return pltpu.CompilerParams(dimension_semantics=sem, vmem_limit_bytes=VMEM_LIMIT)


def _dot(a, b):
    return jnp.dot(a, b, preferred_element_type=F32)


def _dot_nt(a, b):
    return lax.dot_general(a, b, (((1,), (1,)), ((), ())), preferred_element_type=F32)


def _dot_tn(a, b):
    return lax.dot_general(a, b, (((0,), (0,)), ((), ())), preferred_element_type=F32)


def _rms(x, g):
    ms = jnp.mean(x * x, axis=-1, keepdims=True)
    return x * lax.rsqrt(ms + NORM_EPS) * g


def _inproj_kernel(x_ref, g1_ref, w_ref, bd_ref, gq_a_ref, gk_a_ref, gq_b_ref, gk_b_ref,
                   cos_ref, sn_ref, sp_ref,
                   qa_ref, ka_ref, va_ref, qb_ref, kb_ref, vb_ref):
    u = _rms(x_ref[...], g1_ref[...]).astype(BF16)
    proj = _dot(u, w_ref[...])

    def head_norm(t, g_row):
        w = t.shape[1]
        sq = t * t
        hi = sq.astype(BF16)
        lo = (sq - hi.astype(F32)).astype(BF16)
        bd = bd_ref[:w, :w]
        ss = _dot(hi, bd) + _dot(lo, bd)
        return t * lax.rsqrt(ss * (1.0 / HEAD_DIM) + NORM_EPS) * g_row

    def rope(t):
        c, sn, sp = cos_ref[...], sn_ref[...], sp_ref[...]
        outs = []
        for j in range(t.shape[1] // LANES):
            xc = t[:, j * LANES:(j + 1) * LANES]
            nxt = pltpu.roll(xc, LANES - 1, 1)
            prv = pltpu.roll(xc, 1, 1)
            outs.append(xc * c + nxt * sn + prv * sp)
        return outs[0] if len(outs) == 1 else jnp.concatenate(outs, axis=1)

    scale = HEAD_DIM ** -0.5
    o = 0
    qa = proj[:, o:o + NA_WIDTH]; o += NA_WIDTH
    ka = proj[:, o:o + NA_WIDTH]; o += NA_WIDTH
    va = proj[:, o:o + NA_WIDTH]; o += NA_WIDTH
    qb = proj[:, o:o + GQA_WIDTH]; o += GQA_WIDTH
    kb = proj[:, o:o + GQA_KV_WIDTH]; o += GQA_KV_WIDTH
    vb = proj[:, o:o + GQA_KV_WIDTH]

    qa_ref[...] = (head_norm(qa, gq_a_ref[...]) * scale).astype(BF16)
    ka_ref[...] = head_norm(ka, gk_a_ref[...]).astype(BF16)
    va_ref[...] = va.astype(BF16)
    qb_ref[...] = (rope(head_norm(qb, gq_b_ref[...])) * scale).astype(BF16)
    kb_ref[...] = rope(head_norm(kb, gk_b_ref[...])).astype(BF16)
    vb_ref[...] = vb.astype(BF16)


def _inproj(x2, g1, w_bf, bd, gq_a, gk_a, gq_b, gk_b, cos_t, sn_t, sp_t, seq, tm):
    n, d = x2.shape
    in_w = w_bf.shape[1]
    per_seq = seq // tm
    row = lambda i: (i, 0)
    const = lambda i: (0, 0)
    pos = lambda i: (i % per_seq, 0)
    outs = [jax.ShapeDtypeStruct((n, w), BF16)
            for w in (NA_WIDTH, NA_WIDTH, NA_WIDTH, GQA_WIDTH, GQA_KV_WIDTH, GQA_KV_WIDTH)]
    return pl.pallas_call(
        _inproj_kernel,
        grid=(n // tm,),
        in_specs=[
            pl.BlockSpec((tm, d), row),
            pl.BlockSpec((1, d), const),
            pl.BlockSpec((d, in_w), const),
            pl.BlockSpec(bd.shape, const),
            pl.BlockSpec((1, NA_WIDTH), const),
            pl.BlockSpec((1, NA_WIDTH), const),
            pl.BlockSpec((1, GQA_WIDTH), const),
            pl.BlockSpec((1, GQA_KV_WIDTH), const),
            pl.BlockSpec((tm, LANES), pos),
            pl.BlockSpec((tm, LANES), pos),
            pl.BlockSpec((tm, LANES), pos),
        ],
        out_specs=[pl.BlockSpec((tm, s.shape[1]), row) for s in outs],
        out_shape=outs,
        compiler_params=_cparams(("parallel",)),
        name="inproj",
    )(x2, g1, w_bf, bd, gq_a, gk_a, gq_b, gk_b, cos_t, sn_t, sp_t)


def _na_kernel(q_ref, k_ref, v_ref, bias_ref, g_ref, o_ref, *, rows):
    j = pl.program_id(1)
    nq = NA_QROWS * GRID_W
    nk = NA_KROWS * GRID_W
    krow0 = jnp.clip(NA_QROWS * j - NA_WIN_H // 2, 0, rows - NA_KROWS)
    k0 = pl.multiple_of(krow0 * GRID_W, GRID_W)
    nblk = rows // NA_QROWS
    pat = jnp.clip(j, 0, 2) + jnp.clip(j - (nblk - 3), 0, 2)
    lane = lax.broadcasted_iota(jnp.int32, (nq, LANES), 1)
    low = lane < HEAD_DIM
    outs = []
    for c in range(NA_WIDTH // LANES):
        cs = slice(c * LANES, (c + 1) * LANES)
        kc = k_ref[pl.ds(k0, nk), cs]
        vc = v_ref[pl.ds(k0, nk), cs]
        qc = q_ref[:, cs]
        zero = jnp.zeros_like(qc)
        qbd = jnp.concatenate([jnp.where(low, qc, zero), jnp.where(low, zero, qc)], axis=0)
        st = _dot_nt(kc, qbd) + bias_ref[pat, c]
        m = jnp.max(st, axis=0, keepdims=True)
        p = jnp.exp(st - m)
        l = jnp.sum(p, axis=0, keepdims=True)
        ot = _dot_tn(vc, p.astype(BF16)) / l
        o2 = ot.T
        outs.append(jnp.where(low, o2[:nq], o2[nq:]))
    o = jnp.concatenate(outs, axis=1)
    o_ref[...] = _rms(o, g_ref[...]).astype(BF16)


def _na_attn(qa, ka, va, bias, g, batch, seq):
    rows = seq // GRID_W
    nq = NA_QROWS * GRID_W
    nblk = rows // NA_QROWS
    return pl.pallas_call(
        functools.partial(_na_kernel, rows=rows),
        grid=(batch, nblk),
        in_specs=[
            pl.BlockSpec((nq, NA_WIDTH), lambda b, j: (b * nblk + j, 0)),
            pl.BlockSpec((seq, NA_WIDTH), lambda b, j: (b, 0)),
            pl.BlockSpec((seq, NA_WIDTH), lambda b, j: (b, 0)),
            pl.BlockSpec(bias.shape, lambda b, j: (0, 0, 0, 0)),
            pl.BlockSpec((1, NA_WIDTH), lambda b, j: (0, 0)),
        ],
        out_specs=pl.BlockSpec((nq, NA_WIDTH), lambda b, j: (b * nblk + j, 0)),
        out_shape=jax.ShapeDtypeStruct(qa.shape, BF16),
        compiler_params=_cparams(("parallel", "arbitrary")),
        name="na_attn",
    )(qa, ka, va, bias, g)


def _na_bias_table(rpb, rows):
    nblk = rows // NA_QROWS
    pats = [0, 1, 2, nblk - 2, nblk - 1]
    kh, kw = NA_WIN_H, NA_WIN_W
    ridx, cidx, valid = [], [], []
    for j in pats:
        krow0 = int(np.clip(NA_QROWS * j - kh // 2, 0, rows - NA_KROWS))
        kr = krow0 + np.arange(NA_KROWS)[:, None, None, None]
        kc = np.arange(GRID_W)[None, :, None, None]
        qr = NA_QROWS * j + np.arange(NA_QROWS)[None, None, :, None]
        qc = np.arange(GRID_W)[None, None, None, :]
        rs = np.clip(qr - kh // 2, 0, rows - kh)
        cs = np.clip(qc - kw // 2, 0, GRID_W - kw)
        ok = (kr >= rs) & (kr < rs + kh) & (kc >= cs) & (kc < cs + kw)
        ri = np.clip(kr - qr + (NA_WIN_H - 1), 0, 2 * NA_WIN_H - 2)
        ci = np.clip(kc - qc + (NA_WIN_W - 1), 0, 2 * NA_WIN_W - 2)
        shape = (NA_KROWS * GRID_W, NA_QROWS * GRID_W)
        full = np.broadcast_to
        ridx.append(full(ri, ok.shape).reshape(shape))
        cidx.append(full(ci, ok.shape).reshape(shape))
        valid.append(ok.reshape(shape))
    ridx, cidx, valid = np.stack(ridx), np.stack(cidx), np.stack(valid)
    dense = rpb[:, ridx, cidx]
    dense = jnp.where(valid[None], dense, MASK_NEG)
    h, p, nk, nq = dense.shape
    dense = dense.reshape(h // 2, 2, p, nk, nq).transpose(2, 0, 3, 1, 4)
    return dense.reshape(p, h // 2, nk, 2 * nq)


def _gqa_kernel(q_ref, k_ref, v_ref, g_ref, o_ref, *, tq):
    k = k_ref[...]
    v = v_ref[...]
    lane = lax.broadcasted_iota(jnp.int32, (tq, LANES), 1)
    low = lane < HEAD_DIM
    nch = GQA_WIDTH // LANES
    per_group = []
    for grp in range(GQA_KV_HEADS):
        keep = low if grp == 0 else jnp.logical_not(low)
        parts = []
        for c in range(nch):
            qc = q_ref[:, c * LANES:(c + 1) * LANES]
            parts.append(jnp.where(keep, qc, jnp.zeros_like(qc)))
        qs = jnp.concatenate(parts, axis=0)
        st = _dot_nt(k, qs)
        m = jnp.max(st, axis=0, keepdims=True)
        p = jnp.exp(st - m)
        l = jnp.sum(p, axis=0, keepdims=True)
        ot = _dot_tn(v, p.astype(BF16)) / l
        per_group.append(ot.T)
    outs = [jnp.where(low, per_group[0][c * tq:(c + 1) * tq], per_group[1][c * tq:(c + 1) * tq])
            for c in range(nch)]
    o = jnp.concatenate(outs, axis=1)
    o_ref[...] = _rms(o, g_ref[...]).astype(BF16)


def _gqa_attn(qb, kb, vb, g, batch, seq, tq):
    nblk = seq // tq
    return pl.pallas_call(
        functools.partial(_gqa_kernel, tq=tq),
        grid=(batch, nblk),
        in_specs=[
            pl.BlockSpec((tq, GQA_WIDTH), lambda b, i: (b * nblk + i, 0)),
            pl.BlockSpec((seq, GQA_KV_WIDTH), lambda b, i: (b, 0)),
            pl.BlockSpec((seq, GQA_KV_WIDTH), lambda b, i: (b, 0)),
            pl.BlockSpec((1, GQA_WIDTH), lambda b, i: (0, 0)),
        ],
        out_specs=pl.BlockSpec((tq, GQA_WIDTH), lambda b, i: (b * nblk + i, 0)),
        out_shape=jax.ShapeDtypeStruct(qb.shape, BF16),
        compiler_params=_cparams(("parallel", "arbitrary")),
        name="gqa_attn",
    )(qb, kb, vb, g)


def _post_kernel(x_ref, ma_ref, mb_ref, wo_ref, g2_ref, wr_ref, h_ref, hn_ref, aff_ref, *, n_experts):
    attn = _dot(ma_ref[...], wo_ref[:NA_WIDTH, :]) + _dot(mb_ref[...], wo_ref[NA_WIDTH:, :])
    h = x_ref[...] + attn
    h_ref[...] = h
    hn = _rms(h, g2_ref[...]).astype(BF16)
    hn_ref[...] = hn
    logits = _dot(hn, wr_ref[...])
    lane = lax.broadcasted_iota(jnp.int32, logits.shape, 1)
    logits = jnp.where(lane < n_experts, logits, MASK_NEG)
    m = jnp.max(logits, axis=-1, keepdims=True)
    e = jnp.exp(logits - m)
    aff_ref[...] = e / jnp.sum(e, axis=-1, keepdims=True)


def _post(x2, ma, mb, wo_bf, g2, wr_bf, n_experts, tm):
    n, d = x2.shape
    row = lambda i: (i, 0)
    const = lambda i: (0, 0)
    return pl.pallas_call(
        functools.partial(_post_kernel, n_experts=n_experts),
        grid=(n // tm,),
        in_specs=[
            pl.BlockSpec((tm, d), row),
            pl.BlockSpec((tm, NA_WIDTH), row),
            pl.BlockSpec((tm, GQA_WIDTH), row),
            pl.BlockSpec(wo_bf.shape, const),
            pl.BlockSpec((1, d), const),
            pl.BlockSpec(wr_bf.shape, const),
        ],
        out_specs=[pl.BlockSpec((tm, d), row), pl.BlockSpec((tm, d), row), pl.BlockSpec((tm, LANES), row)],
        out_shape=[jax.ShapeDtypeStruct((n, d), F32), jax.ShapeDtypeStruct((n, d), BF16),
                   jax.ShapeDtypeStruct((n, LANES), F32)],
        compiler_params=_cparams(("parallel",)),
        name="post",
    )(x2, ma, mb, wo_bf, g2, wr_bf)


def _route_kernel(aff_ref, hn_ref, x_ref, gate_ref, posT_ref, tri_ref, aT_ref, pos_ref, *, n_experts, cap):
    seq = aff_ref.shape[0]
    rb = 256

    @pl.when(pl.program_id(0) == 0)
    def _():
        col = lax.broadcasted_iota(jnp.int32, (rb, seq), 1)
        row = lax.broadcasted_iota(jnp.int32, (rb, seq), 0)

        def fill(i, carry):
            r0 = pl.multiple_of(i * rb, rb)
            tri_ref[pl.ds(r0, rb), :] = jnp.where(row + r0 < col, 1.0, 0.0).astype(BF16)
            return carry

        lax.fori_loop(0, seq // rb, fill, 0)

    aT = aff_ref[...].T[:n_experts, :]
    def search(i, prefix):
        cand = prefix | (1 << (30 - i))
        cnt = jnp.sum(jnp.where(aT >= pltpu.bitcast(cand, F32), 1.0, 0.0), axis=1, keepdims=True)
        return jnp.where(cnt >= cap, cand, prefix)

    thr_bits = lax.fori_loop(0, 31, search, jnp.zeros((n_experts, 1), jnp.int32))
    thr = pltpu.bitcast(thr_bits, F32)
    gt = aT > thr
    eq = aT == thr
    n_gt = jnp.sum(jnp.where(gt, 1.0, 0.0), axis=1, keepdims=True)
    tri = tri_ref[...]
    eq_rank = _dot(jnp.where(eq, 1.0, 0.0).astype(BF16), tri)
    sel = gt | (eq & (eq_rank < cap - n_gt))
    pos = _dot(jnp.where(sel, 1.0, 0.0).astype(BF16), tri)
    posm = jnp.where(sel, pos, -1.0)
    pos_ref[...] = posm
    aT_ref[...] = aT
    pad = jnp.full((LANES - n_experts, seq), -1.0, F32)
    posT_ref[...] = jnp.concatenate([posm, pad], axis=0).T

    hn = hn_ref[...]
    slot = lax.broadcasted_iota(jnp.int32, (cap, seq), 0).astype(F32)

    def gather(e, carry):
        hit = pos_ref[pl.ds(e, 1), :] == slot
        onehot = jnp.where(hit, 1.0, 0.0).astype(BF16)
        x_ref[e] = _dot(onehot, hn).astype(BF16)
        gate_ref[e] = jnp.sum(jnp.where(hit, aT_ref[pl.ds(e, 1), :], 0.0), axis=1, keepdims=True)
        return carry

    lax.fori_loop(0, n_experts, gather, 0)


def _route(aff, hn, batch, seq, n_experts, cap):
    d = hn.shape[1]
    return pl.pallas_call(
        functools.partial(_route_kernel, n_experts=n_experts, cap=cap),
        grid=(batch,),
        in_specs=[
            pl.BlockSpec((seq, LANES), lambda b: (b, 0)),
            pl.BlockSpec((seq, d), lambda b: (b, 0)),
        ],
        out_specs=[
            pl.BlockSpec((n_experts, cap, d), lambda b: (0, b, 0)),
            pl.BlockSpec((n_experts, cap, 1), lambda b: (0, b, 0)),
            pl.BlockSpec((seq, LANES), lambda b: (b, 0)),
        ],
        out_shape=[
            jax.ShapeDtypeStruct((n_experts, batch * cap, d), BF16),
            jax.ShapeDtypeStruct((n_experts, batch * cap, 1), F32),
            jax.ShapeDtypeStruct((batch * seq, LANES), F32),
        ],
        scratch_shapes=[
            pltpu.VMEM((seq, seq), BF16),
            pltpu.VMEM((n_experts, seq), F32),
            pltpu.VMEM((n_experts, seq), F32),
        ],
        compiler_params=_cparams(("arbitrary",)),
        name="route",
    )(aff, hn)


def _ffn_kernel(x_ref, gate_ref, wg_ref, wu_ref, wd_ref, y_ref, acc_ref):
    f = pl.program_id(2)
    x = x_ref[...]
    g = _dot(x, wg_ref[...].astype(BF16))
    u = _dot(x, wu_ref[...].astype(BF16))
    a = (g * (1.0 / (1.0 + jnp.exp(-g))) * u).astype(BF16)
    contrib = _dot(a, wd_ref[...].astype(BF16))

    @pl.when(f == 0)
    def _():
        acc_ref[...] = contrib

    @pl.when(f > 0)
    def _():
        acc_ref[...] += contrib

    @pl.when(f == pl.num_programs(2) - 1)
    def _():
        y_ref[...] = (acc_ref[...] * gate_ref[...]).astype(BF16)


def _ffn(xg, gate, w_gate, w_up, w_down, tm, tf):
    e, m, d = xg.shape
    dff = w_gate.shape[2]
    return pl.pallas_call(
        _ffn_kernel,
        grid=(e, m // tm, dff // tf),
        in_specs=[
            pl.BlockSpec((None, tm, d), lambda e, i, f: (e, i, 0)),
            pl.BlockSpec((None, tm, 1), lambda e, i, f: (e, i, 0)),
            pl.BlockSpec((None, d, tf), lambda e, i, f: (e, 0, f)),
            pl.BlockSpec((None, d, tf), lambda e, i, f: (e, 0, f)),
            pl.BlockSpec((None, tf, d), lambda e, i, f: (e, f, 0)),
        ],
        out_specs=pl.BlockSpec((None, tm, d), lambda e, i, f: (e, i, 0)),
        out_shape=jax.ShapeDtypeStruct((e, m, d), BF16),
        scratch_shapes=[pltpu.VMEM((tm, d), F32)],
        compiler_params=_cparams(("parallel", "parallel", "arbitrary")),
        name="ffn",
    )(xg, gate, w_gate, w_up, w_down)


def _combine_kernel(h_ref, posT_ref, y_ref, o_ref, *, n_experts, cap):
    tt = h_ref.shape[0]
    post = posT_ref[...]
    slot = lax.broadcasted_iota(jnp.int32, (tt, cap), 1).astype(F32)
    parts = []
    for e in range(n_experts):
        hit = jnp.broadcast_to(post[:, e:e + 1], (tt, cap)) == slot
        parts.append(jnp.where(hit, 1.0, 0.0).astype(BF16))
    onehot = jnp.concatenate(parts, axis=1)
    d = y_ref.shape[2]
    y = y_ref[...].reshape(n_experts * cap, d)
    o_ref[...] = h_ref[...] + _dot(onehot, y)


def _combine(h, posT, y, batch, seq, n_experts, cap, tt):
    n, d = h.shape
    per_seq = seq // tt
    return pl.pallas_call(
        functools.partial(_combine_kernel, n_experts=n_experts, cap=cap),
        grid=(batch, per_seq),
        in_specs=[
            pl.BlockSpec((tt, d), lambda b, i: (b * per_seq + i, 0)),
            pl.BlockSpec((tt, LANES), lambda b, i: (b * per_seq + i, 0)),
            pl.BlockSpec((n_experts, cap, d), lambda b, i: (0, b, 0)),
        ],
        out_specs=pl.BlockSpec((tt, d), lambda b, i: (b * per_seq + i, 0)),
        out_shape=jax.ShapeDtypeStruct((n, d), F32),
        compiler_params=_cparams(("parallel", "arbitrary")),
        name="combine",
    )(h, posT, y)


def _rope_tables(seq):
    t = np.arange(seq)
    row = (t // GRID_W).astype(np.float32)
    col = (t % GRID_W).astype(np.float32)
    half = HEAD_DIM // 2
    inv_freq = 1.0 / (ROPE_THETA ** (jnp.arange(0, half, 2, dtype=F32) / half))
    ang = jnp.concatenate([row[:, None] * inv_freq[None], col[:, None] * inv_freq[None]], axis=-1)
    cos, sin = jnp.cos(ang), jnp.sin(ang)
    lane = np.arange(LANES)
    pair = (lane % HEAD_DIM) // 2
    even = (lane % 2 == 0)[None, :]
    cos_l = cos[:, pair]
    sin_l = sin[:, pair]
    return cos_l, jnp.where(even, -sin_l, 0.0), jnp.where(even, 0.0, sin_l)


def _layer(x2, batch, seq, norm1_g, w_in, na_q_g, na_k_g, na_rpb, gqa_q_g, gqa_k_g,
           na_out_g, gqa_out_g, w_out, norm2_g, w_router, w_gate, w_up, w_down):
    d = x2.shape[1]
    n_experts = w_router.shape[1]
    cap = EC_CAPACITY_FACTOR * seq // n_experts
    rows = seq // GRID_W

    per_kv = GQA_HEADS // GQA_KV_HEADS
    head_order = np.arange(GQA_HEADS).reshape(GQA_KV_HEADS, per_kv).T.reshape(-1)
    perm = (head_order[:, None] * HEAD_DIM + np.arange(HEAD_DIM)[None, :]).reshape(-1)
    qb0 = 3 * NA_WIDTH
    w_in_p = jnp.concatenate([w_in[:, :qb0], w_in[:, qb0 + perm], w_in[:, qb0 + GQA_WIDTH:]], axis=1)
    w_out_p = jnp.concatenate([w_out[:NA_WIDTH], w_out[NA_WIDTH + perm]], axis=0)
    gqa_out_g_p = gqa_out_g[perm]

    hd = np.arange(max(NA_WIDTH, GQA_WIDTH)) // HEAD_DIM
    bd = jnp.asarray(hd[:, None] == hd[None, :], dtype=BF16)
    cos_t, sn_t, sp_t = _rope_tables(seq)
    tile_g = lambda g, heads: jnp.tile(g, heads)[None, :]

    qa, ka, va, qb, kb, vb = _inproj(
        x2, norm1_g[None, :], w_in_p.astype(BF16), bd,
        tile_g(na_q_g, NA_HEADS), tile_g(na_k_g, NA_HEADS),
        tile_g(gqa_q_g, GQA_HEADS), tile_g(gqa_k_g, GQA_KV_HEADS),
        cos_t, sn_t, sp_t, seq, tm=512)

    bias = _na_bias_table(na_rpb, rows)
    ma = _na_attn(qa, ka, va, bias, na_out_g[None, :], batch, seq)
    mb = _gqa_attn(qb, kb, vb, gqa_out_g_p[None, :], batch, seq, tq=256)

    wr = jnp.zeros((d, LANES), F32).at[:, :n_experts].set(w_router).astype(BF16)
    h, hn, aff = _post(x2, ma, mb, w_out_p.astype(BF16), norm2_g[None, :], wr, n_experts, tm=512)

    xg, gate, posT = _route(aff, hn, batch, seq, n_experts, cap)
    y = _ffn(xg, gate, w_gate, w_up, w_down, tm=2048, tf=256)
    return _combine(h, posT, y, batch, seq, n_experts, cap, tt=512)


def kernel(x, norm1_g, w_in, na_q_norm_g, na_k_norm_g, na_rpb, gqa_q_norm_g, gqa_k_norm_g,
           na_out_g, gqa_out_g, w_out, norm2_g, w_router, w_gate, w_up, w_down):
    batch, seq, d = x.shape
    x2 = x.reshape(batch * seq, d)
    for l in range(norm1_g.shape[0]):
        x2 = _layer(x2, batch, seq, norm1_g[l], w_in[l], na_q_norm_g[l], na_k_norm_g[l], na_rpb[l],
                    gqa_q_norm_g[l], gqa_k_norm_g[l], na_out_g[l], gqa_out_g[l], w_out[l],
                    norm2_g[l], w_router[l], w_gate[l], w_up[l], w_down[l])
    return x2.reshape(batch, seq, d)
```

```python
import functools

import jax
import jax.numpy as jnp
import numpy as np
from jax import lax
from jax.experimental import pallas as pl
from jax.experimental.pallas import tpu as pltpu

F32 = jnp.float32
BF16 = jnp.bfloat16

GRID_W = 64
HEAD_DIM = 64
NA_HEADS = 8
NA_WIN_H = 8
NA_WIN_W = 16
GQA_HEADS = 8
GQA_KV_HEADS = 2
ROPE_THETA = 10000.0
N_EXPERTS = 16
EC_CAPACITY_FACTOR = 2
NORM_EPS = 1e-6

LANES = 128
NA_WIDTH = NA_HEADS * HEAD_DIM
GQA_WIDTH = GQA_HEADS * HEAD_DIM
GQA_KV_WIDTH = GQA_KV_HEADS * HEAD_DIM
NA_QROWS = 2
NA_KROWS = NA_WIN_H + NA_QROWS
MASK_NEG = -1e30

VMEM_LIMIT = 56 * 1024 * 1024


def _cparams(sem):
    return pltpu.CompilerParams(dimension_semantics=sem, vmem_limit_bytes=VMEM_LIMIT)


def _dot(a, b):
    return jnp.dot(a, b, preferred_element_type=F32)


def _dot_nt(a, b):
    return lax.dot_general(a, b, (((1,), (1,)), ((), ())), preferred_element_type=F32)


def _dot_tn(a, b):
    return lax.dot_general(a, b, (((0,), (0,)), ((), ())), preferred_element_type=F32)


def _rms(x, g):
    ms = jnp.mean(x * x, axis=-1, keepdims=True)
    return x * lax.rsqrt(ms + NORM_EPS) * g


def _inproj_kernel(x_ref, g1_ref, w_ref, bd_ref, gq_a_ref, gk_a_ref, gq_b_ref, gk_b_ref,
                   cos_ref, sn_ref, sp_ref,
                   qa_ref, ka_ref, va_ref, qb_ref, kb_ref, vb_ref):
    u = _rms(x_ref[...], g1_ref[...]).astype(BF16)
    proj = _dot(u, w_ref[...])

    def head_norm(t, g_row):
        w = t.shape[1]
        sq = t * t
        hi = sq.astype(BF16)
        lo = (sq - hi.astype(F32)).astype(BF16)
        bd = bd_ref[:w, :w]
        ss = _dot(hi, bd) + _dot(lo, bd)
        return t * lax.rsqrt(ss * (1.0 / HEAD_DIM) + NORM_EPS) * g_row

    def rope(t):
        c, sn, sp = cos_ref[...], sn_ref[...], sp_ref[...]
        outs = []
        for j in range(t.shape[1] // LANES):
            xc = t[:, j * LANES:(j + 1) * LANES]
            nxt = pltpu.roll(xc, LANES - 1, 1)
            prv = pltpu.roll(xc, 1, 1)
            outs.append(xc * c + nxt * sn + prv * sp)
        return outs[0] if len(outs) == 1 else jnp.concatenate(outs, axis=1)

    scale = HEAD_DIM ** -0.5
    o = 0
    qa = proj[:, o:o + NA_WIDTH]; o += NA_WIDTH
    ka = proj[:, o:o + NA_WIDTH]; o += NA_WIDTH
    va = proj[:, o:o + NA_WIDTH]; o += NA_WIDTH
    qb = proj[:, o:o + GQA_WIDTH]; o += GQA_WIDTH
    kb = proj[:, o:o + GQA_KV_WIDTH]; o += GQA_KV_WIDTH
    vb = proj[:, o:o + GQA_KV_WIDTH]

    qa_ref[...] = (head_norm(qa, gq_a_ref[...]) * scale).astype(BF16)
    ka_ref[...] = head_norm(ka, gk_a_ref[...]).astype(BF16)
    va_ref[...] = va.astype(BF16)
    qb_ref[...] = (rope(head_norm(qb, gq_b_ref[...])) * scale).astype(BF16)
    kb_ref[...] = rope(head_norm(kb, gk_b_ref[...])).astype(BF16)
    vb_ref[...] = vb.astype(BF16)


def _inproj(x2, g1, w_bf, bd, gq_a, gk_a, gq_b, gk_b, cos_t, sn_t, sp_t, seq, tm):
    n, d = x2.shape
    in_w = w_bf.shape[1]
    per_seq = seq // tm
    row = lambda i: (i, 0)
    const = lambda i: (0, 0)
    pos = lambda i: (i % per_seq, 0)
    outs = [jax.ShapeDtypeStruct((n, w), BF16)
            for w in (NA_WIDTH, NA_WIDTH, NA_WIDTH, GQA_WIDTH, GQA_KV_WIDTH, GQA_KV_WIDTH)]
    return pl.pallas_call(
        _inproj_kernel,
        grid=(n // tm,),
        in_specs=[
            pl.BlockSpec((tm, d), row),
            pl.BlockSpec((1, d), const),
            pl.BlockSpec((d, in_w), const),
            pl.BlockSpec(bd.shape, const),
            pl.BlockSpec((1, NA_WIDTH), const),
            pl.BlockSpec((1, NA_WIDTH), const),
            pl.BlockSpec((1, GQA_WIDTH), const),
            pl.BlockSpec((1, GQA_KV_WIDTH), const),
            pl.BlockSpec((tm, LANES), pos),
            pl.BlockSpec((tm, LANES), pos),
            pl.BlockSpec((tm, LANES), pos),
        ],
        out_specs=[pl.BlockSpec((tm, s.shape[1]), row) for s in outs],
        out_shape=outs,
        compiler_params=_cparams(("parallel",)),
        name="inproj",
    )(x2, g1, w_bf, bd, gq_a, gk_a, gq_b, gk_b, cos_t, sn_t, sp_t)


def _na_kernel(q_ref, k_ref, v_ref, bias_ref, g_ref, o_ref, *, rows):
    j = pl.program_id(1)
    nq = NA_QROWS * GRID_W
    nk = NA_KROWS * GRID_W
    krow0 = jnp.clip(NA_QROWS * j - NA_WIN_H // 2, 0, rows - NA_KROWS)
    k0 = pl.multiple_of(krow0 * GRID_W, GRID_W)
    nblk = rows // NA_QROWS
    pat = jnp.clip(j, 0, 2) + jnp.clip(j - (nblk - 3), 0, 2)
    lane = lax.broadcasted_iota(jnp.int32, (nq, LANES), 1)
    low = lane < HEAD_DIM
    outs = []
    for c in range(NA_WIDTH // LANES):
        cs = slice(c * LANES, (c + 1) * LANES)
        kc = k_ref[pl.ds(k0, nk), cs]
        vc = v_ref[pl.ds(k0, nk), cs]
        qc = q_ref[:, cs]
        zero = jnp.zeros_like(qc)
        qbd = jnp.concatenate([jnp.where(low, qc, zero), jnp.where(low, zero, qc)], axis=0)
        st = _dot_nt(kc, qbd) + bias_ref[pat, c]
        m = jnp.max(st, axis=0, keepdims=True)
        p = jnp.exp(st - m)
        l = jnp.sum(p, axis=0, keepdims=True)
        ot = _dot_tn(vc, p.astype(BF16)) / l
        o2 = ot.T
        outs.append(jnp.where(low, o2[:nq], o2[nq:]))
    o = jnp.concatenate(outs, axis=1)
    o_ref[...] = _rms(o, g_ref[...]).astype(BF16)


def _na_attn(qa, ka, va, bias, g, batch, seq):
    rows = seq // GRID_W
    nq = NA_QROWS * GRID_W
    nblk = rows // NA_QROWS
    return pl.pallas_call(
        functools.partial(_na_kernel, rows=rows),
        grid=(batch, nblk),
        in_specs=[
            pl.BlockSpec((nq, NA_WIDTH), lambda b, j: (b * nblk + j, 0)),
            pl.BlockSpec((seq, NA_WIDTH), lambda b, j: (b, 0)),
            pl.BlockSpec((seq, NA_WIDTH), lambda b, j: (b, 0)),
            pl.BlockSpec(bias.shape, lambda b, j: (0, 0, 0, 0)),
            pl.BlockSpec((1, NA_WIDTH), lambda b, j: (0, 0)),
        ],
        out_specs=pl.BlockSpec((nq, NA_WIDTH), lambda b, j: (b * nblk + j, 0)),
        out_shape=jax.ShapeDtypeStruct(qa.shape, BF16),
        compiler_params=_cparams(("parallel", "arbitrary")),
        name="na_attn",
    )(qa, ka, va, bias, g)


def _na_bias_table(rpb, rows):
    nblk = rows // NA_QROWS
    pats = [0, 1, 2, nblk - 2, nblk - 1]
    kh, kw = NA_WIN_H, NA_WIN_W
    nh, nrel_r, nrel_c = rpb.shape
    kc = np.arange(GRID_W)[:, None]
    qc = np.arange(GRID_W)[None, :]
    cs = np.clip(qc - kw // 2, 0, GRID_W - kw)
    col_ok = (kc >= cs) & (kc < cs + kw)
    ci = np.clip(kc - qc + (kw - 1), 0, nrel_c - 1)
    pick = (ci[None] == np.arange(nrel_c)[:, None, None]) & col_ok[None]
    toep = jnp.einsum('hdj,jkq->hdkq', rpb, jnp.asarray(pick, F32), precision=lax.Precision.HIGHEST)
    toep = jnp.where(col_ok[None, None], toep, MASK_NEG)
    masked = jnp.full((nh, GRID_W, GRID_W), MASK_NEG, F32)
    pat_blocks = []
    for j in pats:
        krow0 = int(np.clip(NA_QROWS * j - kh // 2, 0, rows - NA_KROWS))
        key_rows = []
        for kr in range(krow0, krow0 + NA_KROWS):
            per_q = []
            for qr in range(NA_QROWS * j, NA_QROWS * (j + 1)):
                rs = int(np.clip(qr - kh // 2, 0, rows - kh))
                per_q.append(toep[:, kr - qr + kh - 1] if rs <= kr < rs + kh else masked)
            key_rows.append(jnp.stack(per_q, axis=2))
        pat_blocks.append(jnp.stack(key_rows, axis=1))
    dense = jnp.stack(pat_blocks, axis=0)
    p = len(pats)
    nk, nq = NA_KROWS * GRID_W, NA_QROWS * GRID_W
    dense = dense.reshape(p, nh // 2, 2, nk, nq).transpose(0, 1, 3, 2, 4)
    return dense.reshape(p, nh // 2, nk, 2 * nq)


def _gqa_kernel(q_ref, k_ref, v_ref, g_ref, o_ref, *, tq):
    k = k_ref[...]
    v = v_ref[...]
    lane = lax.broadcasted_iota(jnp.int32, (tq, LANES), 1)
    low = lane < HEAD_DIM
    nch = GQA_WIDTH // LANES
    per_group = []
    for grp in range(GQA_KV_HEADS):
        keep = low if grp == 0 else jnp.logical_not(low)
        parts = []
        for c in range(nch):
            qc = q_ref[:, c * LANES:(c + 1) * LANES]
            parts.append(jnp.where(keep, qc, jnp.zeros_like(qc)))
        qs = jnp.concatenate(parts, axis=0)
        st = _dot_nt(k, qs)
        m = jnp.max(st, axis=0, keepdims=True)
        p = jnp.exp(st - m)
        l = jnp.sum(p, axis=0, keepdims=True)
        ot = _dot_tn(v, p.astype(BF16)) / l
        per_group.append(ot.T)
    outs = [jnp.where(low, per_group[0][c * tq:(c + 1) * tq], per_group[1][c * tq:(c + 1) * tq])
            for c in range(nch)]
    o = jnp.concatenate(outs, axis=1)
    o_ref[...] = _rms(o, g_ref[...]).astype(BF16)


def _gqa_attn(qb, kb, vb, g, batch, seq, tq):
    nblk = seq // tq
    return pl.pallas_call(
        functools.partial(_gqa_kernel, tq=tq),
        grid=(batch, nblk),
        in_specs=[
            pl.BlockSpec((tq, GQA_WIDTH), lambda b, i: (b * nblk + i, 0)),
            pl.BlockSpec((seq, GQA_KV_WIDTH), lambda b, i: (b, 0)),
            pl.BlockSpec((seq, GQA_KV_WIDTH), lambda b, i: (b, 0)),
            pl.BlockSpec((1, GQA_WIDTH), lambda b, i: (0, 0)),
        ],
        out_specs=pl.BlockSpec((tq, GQA_WIDTH), lambda b, i: (b * nblk + i, 0)),
        out_shape=jax.ShapeDtypeStruct(qb.shape, BF16),
        compiler_params=_cparams(("parallel", "arbitrary")),
        name="gqa_attn",
    )(qb, kb, vb, g)


def _post_kernel(x_ref, ma_ref, mb_ref, wo_ref, g2_ref, wr_ref, h_ref, hn_ref, aff_ref, *, n_experts):
    attn = _dot(ma_ref[...], wo_ref[:NA_WIDTH, :]) + _dot(mb_ref[...], wo_ref[NA_WIDTH:, :])
    h = x_ref[...] + attn
    h_ref[...] = h
    hn = _rms(h, g2_ref[...]).astype(BF16)
    hn_ref[...] = hn
    logits = _dot(hn, wr_ref[...])
    lane = lax.broadcasted_iota(jnp.int32, logits.shape, 1)
    logits = jnp.where(lane < n_experts, logits, MASK_NEG)
    m = jnp.max(logits, axis=-1, keepdims=True)
    e = jnp.exp(logits - m)
    aff_ref[...] = e / jnp.sum(e, axis=-1, keepdims=True)


def _post(x2, ma, mb, wo_bf, g2, wr_bf, n_experts, tm):
    n, d = x2.shape
    row = lambda i: (i, 0)
    const = lambda i: (0, 0)
    return pl.pallas_call(
        functools.partial(_post_kernel, n_experts=n_experts),
        grid=(n // tm,),
        in_specs=[
            pl.BlockSpec((tm, d), row),
            pl.BlockSpec((tm, NA_WIDTH), row),
            pl.BlockSpec((tm, GQA_WIDTH), row),
            pl.BlockSpec(wo_bf.shape, const),
            pl.BlockSpec((1, d), const),
            pl.BlockSpec(wr_bf.shape, const),
        ],
        out_specs=[pl.BlockSpec((tm, d), row), pl.BlockSpec((tm, d), row), pl.BlockSpec((tm, LANES), row)],
        out_shape=[jax.ShapeDtypeStruct((n, d), F32), jax.ShapeDtypeStruct((n, d), BF16),
                   jax.ShapeDtypeStruct((n, LANES), F32)],
        compiler_params=_cparams(("parallel",)),
        name="post",
    )(x2, ma, mb, wo_bf, g2, wr_bf)


def _route_kernel(aff_ref, hn_ref, x_ref, gate_ref, posT_ref, tri_ref, aT_ref, pos_ref, *, n_experts, cap):
    seq = aff_ref.shape[0]
    rb = 256

    @pl.when(pl.program_id(0) == 0)
    def _():
        col = lax.broadcasted_iota(jnp.int32, (rb, seq), 1)
        row = lax.broadcasted_iota(jnp.int32, (rb, seq), 0)

        def fill(i, carry):
            r0 = pl.multiple_of(i * rb, rb)
            tri_ref[pl.ds(r0, rb), :] = jnp.where(row + r0 < col, 1.0, 0.0).astype(BF16)
            return carry

        lax.fori_loop(0, seq // rb, fill, 0)

    aT = aff_ref[...].T[:n_experts, :]
    def search(i, prefix):
        cand = prefix | (1 << (30 - i))
        cnt = jnp.sum(jnp.where(aT >= pltpu.bitcast(cand, F32), 1.0, 0.0), axis=1, keepdims=True)
        return jnp.where(cnt >= cap, cand, prefix)

    thr_bits = lax.fori_loop(0, 31, search, jnp.zeros((n_experts, 1), jnp.int32))
    thr = pltpu.bitcast(thr_bits, F32)
    gt = aT > thr
    eq = aT == thr
    n_gt = jnp.sum(jnp.where(gt, 1.0, 0.0), axis=1, keepdims=True)
    tri = tri_ref[...]
    eq_rank = _dot(jnp.where(eq, 1.0, 0.0).astype(BF16), tri)
    sel = gt | (eq & (eq_rank < cap - n_gt))
    pos = _dot(jnp.where(sel, 1.0, 0.0).astype(BF16), tri)
    posm = jnp.where(sel, pos, -1.0)
    pos_ref[...] = posm
    aT_ref[...] = aT
    pad = jnp.full((LANES - n_experts, seq), -1.0, F32)
    posT_ref[...] = jnp.concatenate([posm, pad], axis=0).T

    hn = hn_ref[...]
    slot = lax.broadcasted_iota(jnp.int32, (cap, seq), 0).astype(F32)

    def gather(e, carry):
        hit = pos_ref[pl.ds(e, 1), :] == slot
        onehot = jnp.where(hit, 1.0, 0.0).astype(BF16)
        x_ref[e] = _dot(onehot, hn).astype(BF16)
        gate_ref[e] = jnp.sum(jnp.where(hit, aT_ref[pl.ds(e, 1), :], 0.0), axis=1, keepdims=True)
        return carry

    lax.fori_loop(0, n_experts, gather, 0)


def _route(aff, hn, batch, seq, n_experts, cap):
    d = hn.shape[1]
    return pl.pallas_call(
        functools.partial(_route_kernel, n_experts=n_experts, cap=cap),
        grid=(batch,),
        in_specs=[
            pl.BlockSpec((seq, LANES), lambda b: (b, 0)),
            pl.BlockSpec((seq, d), lambda b: (b, 0)),
        ],
        out_specs=[
            pl.BlockSpec((n_experts, cap, d), lambda b: (0, b, 0)),
            pl.BlockSpec((n_experts, cap, 1), lambda b: (0, b, 0)),
            pl.BlockSpec((seq, LANES), lambda b: (b, 0)),
        ],
        out_shape=[
            jax.ShapeDtypeStruct((n_experts, batch * cap, d), BF16),
            jax.ShapeDtypeStruct((n_experts, batch * cap, 1), F32),
            jax.ShapeDtypeStruct((batch * seq, LANES), F32),
        ],
        scratch_shapes=[
            pltpu.VMEM((seq, seq), BF16),
            pltpu.VMEM((n_experts, seq), F32),
            pltpu.VMEM((n_experts, seq), F32),
        ],
        compiler_params=_cparams(("arbitrary",)),
        name="route",
    )(aff, hn)


def _ffn_kernel(x_ref, gate_ref, wg_ref, wu_ref, wd_ref, y_ref, acc_ref):
    f = pl.program_id(2)
    x = x_ref[...]
    g = _dot(x, wg_ref[...].astype(BF16))
    u = _dot(x, wu_ref[...].astype(BF16))
    a = (g * (1.0 / (1.0 + jnp.exp(-g))) * u).astype(BF16)
    contrib = _dot(a, wd_ref[...].astype(BF16))

    @pl.when(f == 0)
    def _():
        acc_ref[...] = contrib

    @pl.when(f > 0)
    def _():
        acc_ref[...] += contrib

    @pl.when(f == pl.num_programs(2) - 1)
    def _():
        y_ref[...] = (acc_ref[...] * gate_ref[...]).astype(BF16)


def _ffn(xg, gate, w_gate, w_up, w_down, tm, tf):
    e, m, d = xg.shape
    dff = w_gate.shape[2]
    return pl.pallas_call(
        _ffn_kernel,
        grid=(e, m // tm, dff // tf),
        in_specs=[
            pl.BlockSpec((None, tm, d), lambda e, i, f: (e, i, 0)),
            pl.BlockSpec((None, tm, 1), lambda e, i, f: (e, i, 0)),
            pl.BlockSpec((None, d, tf), lambda e, i, f: (e, 0, f)),
            pl.BlockSpec((None, d, tf), lambda e, i, f: (e, 0, f)),
            pl.BlockSpec((None, tf, d), lambda e, i, f: (e, f, 0)),
        ],
        out_specs=pl.BlockSpec((None, tm, d), lambda e, i, f: (e, i, 0)),
        out_shape=jax.ShapeDtypeStruct((e, m, d), BF16),
        scratch_shapes=[pltpu.VMEM((tm, d), F32)],
        compiler_params=_cparams(("parallel", "parallel", "arbitrary")),
        name="ffn",
    )(xg, gate, w_gate, w_up, w_down)


def _combine_kernel(h_ref, posT_ref, y_ref, o_ref, *, n_experts, cap):
    tt = h_ref.shape[0]
    post = posT_ref[...]
    slot = lax.broadcasted_iota(jnp.int32, (tt, cap), 1).astype(F32)
    parts = []
    for e in range(n_experts):
        hit = jnp.broadcast_to(post[:, e:e + 1], (tt, cap)) == slot
        parts.append(jnp.where(hit, 1.0, 0.0).astype(BF16))
    onehot = jnp.concatenate(parts, axis=1)
    d = y_ref.shape[2]
    y = y_ref[...].reshape(n_experts * cap, d)
    o_ref[...] = h_ref[...] + _dot(onehot, y)


def _combine(h, posT, y, batch, seq, n_experts, cap, tt):
    n, d = h.shape
    per_seq = seq // tt
    return pl.pallas_call(
        functools.partial(_combine_kernel, n_experts=n_experts, cap=cap),
        grid=(batch, per_seq),
        in_specs=[
            pl.BlockSpec((tt, d), lambda b, i: (b * per_seq + i, 0)),
            pl.BlockSpec((tt, LANES), lambda b, i: (b * per_seq + i, 0)),
            pl.BlockSpec((n_experts, cap, d), lambda b, i: (0, b, 0)),
        ],
        out_specs=pl.BlockSpec((tt, d), lambda b, i: (b * per_seq + i, 0)),
        out_shape=jax.ShapeDtypeStruct((n, d), F32),
        compiler_params=_cparams(("parallel", "arbitrary")),
        name="combine",
    )(h, posT, y)


def _rope_tables(seq):
    t = np.arange(seq)
    row = (t // GRID_W).astype(np.float32)
    col = (t % GRID_W).astype(np.float32)
    half = HEAD_DIM // 2
    inv_freq = 1.0 / (ROPE_THETA ** (jnp.arange(0, half, 2, dtype=F32) / half))
    ang = jnp.concatenate([row[:, None] * inv_freq[None], col[:, None] * inv_freq[None]], axis=-1)
    cos, sin = jnp.cos(ang), jnp.sin(ang)
    lane = np.arange(LANES)
    pair = (lane % HEAD_DIM) // 2
    even = (lane % 2 == 0)[None, :]
    cos_l = cos[:, pair]
    sin_l = sin[:, pair]
    return cos_l, jnp.where(even, -sin_l, 0.0), jnp.where(even, 0.0, sin_l)


def _layer(x2, batch, seq, norm1_g, w_in, na_q_g, na_k_g, na_rpb, gqa_q_g, gqa_k_g,
           na_out_g, gqa_out_g, w_out, norm2_g, w_router, w_gate, w_up, w_down):
    d = x2.shape[1]
    n_experts = w_router.shape[1]
    cap = EC_CAPACITY_FACTOR * seq // n_experts
    rows = seq // GRID_W

    per_kv = GQA_HEADS // GQA_KV_HEADS
    head_order = np.arange(GQA_HEADS).reshape(GQA_KV_HEADS, per_kv).T.reshape(-1)
    perm = (head_order[:, None] * HEAD_DIM + np.arange(HEAD_DIM)[None, :]).reshape(-1)
    qb0 = 3 * NA_WIDTH
    w_in_p = jnp.concatenate([w_in[:, :qb0], w_in[:, qb0 + perm], w_in[:, qb0 + GQA_WIDTH:]], axis=1)
    w_out_p = jnp.concatenate([w_out[:NA_WIDTH], w_out[NA_WIDTH + perm]], axis=0)
    gqa_out_g_p = gqa_out_g[perm]

    hd = np.arange(max(NA_WIDTH, GQA_WIDTH)) // HEAD_DIM
    bd = jnp.asarray(hd[:, None] == hd[None, :], dtype=BF16)
    cos_t, sn_t, sp_t = _rope_tables(seq)
    tile_g = lambda g, heads: jnp.tile(g, heads)[None, :]

    qa, ka, va, qb, kb, vb = _inproj(
        x2, norm1_g[None, :], w_in_p.astype(BF16), bd,
        tile_g(na_q_g, NA_HEADS), tile_g(na_k_g, NA_HEADS),
        tile_g(gqa_q_g, GQA_HEADS), tile_g(gqa_k_g, GQA_KV_HEADS),
        cos_t, sn_t, sp_t, seq, tm=512)

    bias = _na_bias_table(na_rpb, rows)
    ma = _na_attn(qa, ka, va, bias, na_out_g[None, :], batch, seq)
    mb = _gqa_attn(qb, kb, vb, gqa_out_g_p[None, :], batch, seq, tq=256)

    wr = jnp.zeros((d, LANES), F32).at[:, :n_experts].set(w_router).astype(BF16)
    h, hn, aff = _post(x2, ma, mb, w_out_p.astype(BF16), norm2_g[None, :], wr, n_experts, tm=512)

    xg, gate, posT = _route(aff, hn, batch, seq, n_experts, cap)
    y = _ffn(xg, gate, w_gate, w_up, w_down, tm=2048, tf=256)
    return _combine(h, posT, y, batch, seq, n_experts, cap, tt=512)


def kernel(x, norm1_g, w_in, na_q_norm_g, na_k_norm_g, na_rpb, gqa_q_norm_g, gqa_k_norm_g,
           na_out_g, gqa_out_g, w_out, norm2_g, w_router, w_gate, w_up, w_down):
    batch, seq, d = x.shape
    x2 = x.reshape(batch * seq, d)
    for l in range(norm1_g.shape[0]):
        x2 = _layer(x2, batch, seq, norm1_g[l], w_in[l], na_q_norm_g[l], na_k_norm_g[l], na_rpb[l],
                    gqa_q_norm_g[l], gqa_k_norm_g[l], na_out_g[l], gqa_out_g[l], w_out[l],
                    norm2_g[l], w_router[l], w_gate[l], w_up[l], w_down[l])
    return x2.reshape(batch, seq, d)
```

```python
import functools

import jax
import jax.numpy as jnp
import numpy as np
from jax import lax
from jax.experimental import pallas as pl
from jax.experimental.pallas import tpu as pltpu

F32 = jnp.float32
BF16 = jnp.bfloat16

GRID_W = 64
HEAD_DIM = 64
NA_HEADS = 8
NA_WIN_H = 8
NA_WIN_W = 16
GQA_HEADS = 8
GQA_KV_HEADS = 2
ROPE_THETA = 10000.0
N_EXPERTS = 16
EC_CAPACITY_FACTOR = 2
NORM_EPS = 1e-6

LANES = 128
NA_WIDTH = NA_HEADS * HEAD_DIM
GQA_WIDTH = GQA_HEADS * HEAD_DIM
GQA_KV_WIDTH = GQA_KV_HEADS * HEAD_DIM
NA_QROWS = 2
NA_KROWS = NA_WIN_H + NA_QROWS
MASK_NEG = -1e30
LOG2E = 1.4426950408889634

VMEM_LIMIT = 56 * 1024 * 1024


def _cparams(sem):
    return pltpu.CompilerParams(dimension_semantics=sem, vmem_limit_bytes=VMEM_LIMIT)


def _dot(a, b):
    return jnp.dot(a, b, preferred_element_type=F32)


def _dot_nt(a, b):
    return lax.dot_general(a, b, (((1,), (1,)), ((), ())), preferred_element_type=F32)


def _dot_tn(a, b):
    return lax.dot_general(a, b, (((0,), (0,)), ((), ())), preferred_element_type=F32)


def _software_pipeline(n, first, second, depth=1):
    pending = [first(i) for i in range(min(depth, n))]
    outs = []
    for i in range(n):
        if i + depth < n:
            pending.append(first(i + depth))
        outs.append(second(i, pending.pop(0)))
    return outs


def _rms(x, g):
    ms = jnp.mean(x * x, axis=-1, keepdims=True)
    return x * lax.rsqrt(ms + NORM_EPS) * g


def _inproj_kernel(x_ref, g1_ref, w_ref, bd_ref, gq_a_ref, gk_a_ref, gq_b_ref, gk_b_ref,
                   cos_ref, sn_ref, sp_ref,
                   qa_ref, ka_ref, va_ref, qb_ref, kb_ref, vb_ref):
    u = _rms(x_ref[...], g1_ref[...]).astype(BF16)
    proj = _dot(u, w_ref[...])

    def head_norm(t, g_row):
        w = t.shape[1]
        sq = t * t
        hi = sq.astype(BF16)
        lo = (sq - hi.astype(F32)).astype(BF16)
        bd = bd_ref[:w, :w]
        ss = _dot(hi, bd) + _dot(lo, bd)
        return t * lax.rsqrt(ss * (1.0 / HEAD_DIM) + NORM_EPS) * g_row

    def rope(t):
        c, sn, sp = cos_ref[...], sn_ref[...], sp_ref[...]
        outs = []
        for j in range(t.shape[1] // LANES):
            xc = t[:, j * LANES:(j + 1) * LANES]
            nxt = pltpu.roll(xc, LANES - 1, 1)
            prv = pltpu.roll(xc, 1, 1)
            outs.append(xc * c + nxt * sn + prv * sp)
        return outs[0] if len(outs) == 1 else jnp.concatenate(outs, axis=1)

    scale = HEAD_DIM ** -0.5 * LOG2E
    o = 0
    qa = proj[:, o:o + NA_WIDTH]; o += NA_WIDTH
    ka = proj[:, o:o + NA_WIDTH]; o += NA_WIDTH
    va = proj[:, o:o + NA_WIDTH]; o += NA_WIDTH
    qb = proj[:, o:o + GQA_WIDTH]; o += GQA_WIDTH
    kb = proj[:, o:o + GQA_KV_WIDTH]; o += GQA_KV_WIDTH
    vb = proj[:, o:o + GQA_KV_WIDTH]

    qa_ref[...] = (head_norm(qa, gq_a_ref[...]) * scale).astype(BF16)
    ka_ref[...] = head_norm(ka, gk_a_ref[...]).astype(BF16)
    va_ref[...] = va.astype(BF16)
    qb_ref[...] = (rope(head_norm(qb, gq_b_ref[...])) * scale).astype(BF16)
    kb_ref[...] = rope(head_norm(kb, gk_b_ref[...])).astype(BF16)
    vb_ref[...] = vb.astype(BF16)


def _inproj(x2, g1, w_bf, bd, gq_a, gk_a, gq_b, gk_b, cos_t, sn_t, sp_t, seq, tm):
    n, d = x2.shape
    in_w = w_bf.shape[1]
    per_seq = seq // tm
    row = lambda i: (i, 0)
    const = lambda i: (0, 0)
    pos = lambda i: (i % per_seq, 0)
    outs = [jax.ShapeDtypeStruct((n, w), BF16)
            for w in (NA_WIDTH, NA_WIDTH, NA_WIDTH, GQA_WIDTH, GQA_KV_WIDTH, GQA_KV_WIDTH)]
    return pl.pallas_call(
        _inproj_kernel,
        grid=(n // tm,),
        in_specs=[
            pl.BlockSpec((tm, d), row),
            pl.BlockSpec((1, d), const),
            pl.BlockSpec((d, in_w), const),
            pl.BlockSpec(bd.shape, const),
            pl.BlockSpec((1, NA_WIDTH), const),
            pl.BlockSpec((1, NA_WIDTH), const),
            pl.BlockSpec((1, GQA_WIDTH), const),
            pl.BlockSpec((1, GQA_KV_WIDTH), const),
            pl.BlockSpec((tm, LANES), pos),
            pl.BlockSpec((tm, LANES), pos),
            pl.BlockSpec((tm, LANES), pos),
        ],
        out_specs=[pl.BlockSpec((tm, s.shape[1]), row) for s in outs],
        out_shape=outs,
        compiler_params=_cparams(("parallel",)),
        name="inproj",
    )(x2, g1, w_bf, bd, gq_a, gk_a, gq_b, gk_b, cos_t, sn_t, sp_t)


def _na_kernel(q_ref, k_ref, v_ref, bias_ref, g_ref, o_ref, *, rows):
    j = pl.program_id(1)
    nq = NA_QROWS * GRID_W
    nk = NA_KROWS * GRID_W
    krow0 = jnp.clip(NA_QROWS * j - NA_WIN_H // 2, 0, rows - NA_KROWS)
    k0 = pl.multiple_of(krow0 * GRID_W, GRID_W)
    nblk = rows // NA_QROWS
    pat = jnp.clip(j, 0, 2) + jnp.clip(j - (nblk - 3), 0, 2)
    low = lax.broadcasted_iota(jnp.int32, (nq, LANES), 1) < HEAD_DIM
    top = lax.broadcasted_iota(jnp.int32, (LANES, nq), 0) < HEAD_DIM

    def scores(c):
        cs = slice(c * LANES, (c + 1) * LANES)
        kc = k_ref[pl.ds(k0, nk), cs]
        qc = q_ref[:, cs]
        zero = jnp.zeros_like(qc)
        qbd = jnp.concatenate([jnp.where(low, qc, zero), jnp.where(low, zero, qc)], axis=0)
        return _dot_nt(kc, qbd) + bias_ref[pat, c]

    def attend(c, st):
        m = jnp.max(st, axis=0, keepdims=True)
        p = jnp.exp2(st - m)
        l = jnp.sum(p, axis=0, keepdims=True)
        vc = v_ref[pl.ds(k0, nk), c * LANES:(c + 1) * LANES]
        ot = _dot_tn(vc, p.astype(BF16)) / l
        return jnp.where(top, ot[:, :nq], ot[:, nq:]).T

    o = jnp.concatenate(_software_pipeline(NA_WIDTH // LANES, scores, attend), axis=1)
    o_ref[...] = _rms(o, g_ref[...]).astype(BF16)


def _na_attn(qa, ka, va, bias, g, batch, seq):
    rows = seq // GRID_W
    nq = NA_QROWS * GRID_W
    nblk = rows // NA_QROWS
    return pl.pallas_call(
        functools.partial(_na_kernel, rows=rows),
        grid=(batch, nblk),
        in_specs=[
            pl.BlockSpec((nq, NA_WIDTH), lambda b, j: (b * nblk + j, 0)),
            pl.BlockSpec((seq, NA_WIDTH), lambda b, j: (b, 0)),
            pl.BlockSpec((seq, NA_WIDTH), lambda b, j: (b, 0)),
            pl.BlockSpec(bias.shape, lambda b, j: (0, 0, 0, 0)),
            pl.BlockSpec((1, NA_WIDTH), lambda b, j: (0, 0)),
        ],
        out_specs=pl.BlockSpec((nq, NA_WIDTH), lambda b, j: (b * nblk + j, 0)),
        out_shape=jax.ShapeDtypeStruct(qa.shape, BF16),
        compiler_params=_cparams(("parallel", "arbitrary")),
        name="na_attn",
    )(qa, ka, va, bias, g)


def _na_bias_table(rpb, rows):
    nblk = rows // NA_QROWS
    pats = [0, 1, 2, nblk - 2, nblk - 1]
    kh, kw = NA_WIN_H, NA_WIN_W
    nh, nrel_r, nrel_c = rpb.shape
    kc = np.arange(GRID_W)[:, None]
    qc = np.arange(GRID_W)[None, :]
    cs = np.clip(qc - kw // 2, 0, GRID_W - kw)
    col_ok = (kc >= cs) & (kc < cs + kw)
    ci = np.clip(kc - qc + (kw - 1), 0, nrel_c - 1)
    pick = (ci[None] == np.arange(nrel_c)[:, None, None]) & col_ok[None]
    toep = jnp.einsum('hdj,jkq->hdkq', rpb, jnp.asarray(pick, F32), precision=lax.Precision.HIGHEST)
    toep = jnp.where(col_ok[None, None], toep * LOG2E, MASK_NEG)
    masked = jnp.full((nh, GRID_W, GRID_W), MASK_NEG, F32)
    pat_blocks = []
    for j in pats:
        krow0 = int(np.clip(NA_QROWS * j - kh // 2, 0, rows - NA_KROWS))
        key_rows = []
        for kr in range(krow0, krow0 + NA_KROWS):
            per_q = []
            for qr in range(NA_QROWS * j, NA_QROWS * (j + 1)):
                rs = int(np.clip(qr - kh // 2, 0, rows - kh))
                per_q.append(toep[:, kr - qr + kh - 1] if rs <= kr < rs + kh else masked)
            key_rows.append(jnp.stack(per_q, axis=2))
        pat_blocks.append(jnp.stack(key_rows, axis=1))
    dense = jnp.stack(pat_blocks, axis=0)
    p = len(pats)
    nk, nq = NA_KROWS * GRID_W, NA_QROWS * GRID_W
    dense = dense.reshape(p, nh // 2, 2, nk, nq).transpose(0, 1, 3, 2, 4)
    return dense.reshape(p, nh // 2, nk, 2 * nq)


def _gqa_kernel(q_ref, k_ref, v_ref, g_ref, o_ref, *, tq):
    k = k_ref[...]
    v = v_ref[...]
    kv_low = lax.broadcasted_iota(jnp.int32, v.shape, 1) < HEAD_DIM
    one = jnp.ones_like(v)
    v_aug = (jnp.where(kv_low, v, one), jnp.where(kv_low, one, v))
    low = lax.broadcasted_iota(jnp.int32, (tq, LANES), 1) < HEAD_DIM
    top = lax.broadcasted_iota(jnp.int32, (LANES, tq), 0) < HEAD_DIM
    nch = GQA_WIDTH // LANES

    def scores(i):
        c, grp = divmod(i, GQA_KV_HEADS)
        qc = q_ref[:, c * LANES:(c + 1) * LANES]
        keep = low if grp == 0 else jnp.logical_not(low)
        return _dot_nt(k, jnp.where(keep, qc, jnp.zeros_like(qc)))

    def attend(i, st):
        grp = i % GQA_KV_HEADS
        m = jnp.max(st, axis=0, keepdims=True)
        p = jnp.exp2(st - m).astype(BF16)
        return _dot_tn(v_aug[grp], p)

    ots = _software_pipeline(nch * GQA_KV_HEADS, scores, attend, depth=2)
    outs = []
    for c in range(nch):
        o0, o1 = ots[GQA_KV_HEADS * c], ots[GQA_KV_HEADS * c + 1]
        l0 = o0[HEAD_DIM:HEAD_DIM + 1, :]
        l1 = o1[0:1, :]
        outs.append(jnp.where(top, o0 / l0, o1 / l1).T)
    o = jnp.concatenate(outs, axis=1)
    o_ref[...] = _rms(o, g_ref[...]).astype(BF16)


def _gqa_attn(qb, kb, vb, g, batch, seq, tq):
    nblk = seq // tq
    return pl.pallas_call(
        functools.partial(_gqa_kernel, tq=tq),
        grid=(batch, nblk),
        in_specs=[
            pl.BlockSpec((tq, GQA_WIDTH), lambda b, i: (b * nblk + i, 0)),
            pl.BlockSpec((seq, GQA_KV_WIDTH), lambda b, i: (b, 0)),
            pl.BlockSpec((seq, GQA_KV_WIDTH), lambda b, i: (b, 0)),
            pl.BlockSpec((1, GQA_WIDTH), lambda b, i: (0, 0)),
        ],
        out_specs=pl.BlockSpec((tq, GQA_WIDTH), lambda b, i: (b * nblk + i, 0)),
        out_shape=jax.ShapeDtypeStruct(qb.shape, BF16),
        compiler_params=_cparams(("parallel", "arbitrary")),
        name="gqa_attn",
    )(qb, kb, vb, g)


def _post_kernel(x_ref, ma_ref, mb_ref, wo_ref, g2_ref, wr_ref, h_ref, hn_ref, aff_ref, *, n_experts):
    attn = _dot(ma_ref[...], wo_ref[:NA_WIDTH, :]) + _dot(mb_ref[...], wo_ref[NA_WIDTH:, :])
    h = x_ref[...] + attn
    h_ref[...] = h
    hn = _rms(h, g2_ref[...]).astype(BF16)
    hn_ref[...] = hn
    logits = _dot(hn, wr_ref[...])
    lane = lax.broadcasted_iota(jnp.int32, logits.shape, 1)
    logits = jnp.where(lane < n_experts, logits, MASK_NEG)
    m = jnp.max(logits, axis=-1, keepdims=True)
    e = jnp.exp(logits - m)
    aff_ref[...] = e / jnp.sum(e, axis=-1, keepdims=True)


def _post(x2, ma, mb, wo_bf, g2, wr_bf, n_experts, tm):
    n, d = x2.shape
    row = lambda i: (i, 0)
    const = lambda i: (0, 0)
    return pl.pallas_call(
        functools.partial(_post_kernel, n_experts=n_experts),
        grid=(n // tm,),
        in_specs=[
            pl.BlockSpec((tm, d), row),
            pl.BlockSpec((tm, NA_WIDTH), row),
            pl.BlockSpec((tm, GQA_WIDTH), row),
            pl.BlockSpec(wo_bf.shape, const),
            pl.BlockSpec((1, d), const),
            pl.BlockSpec(wr_bf.shape, const),
        ],
        out_specs=[pl.BlockSpec((tm, d), row), pl.BlockSpec((tm, d), row), pl.BlockSpec((tm, LANES), row)],
        out_shape=[jax.ShapeDtypeStruct((n, d), F32), jax.ShapeDtypeStruct((n, d), BF16),
                   jax.ShapeDtypeStruct((n, LANES), F32)],
        compiler_params=_cparams(("parallel",)),
        name="post",
    )(x2, ma, mb, wo_bf, g2, wr_bf)


def _route_kernel(aff_ref, hn_ref, x_ref, gate_ref, posT_ref, tri_ref, aT_ref, pos_ref, *, n_experts, cap):
    seq = aff_ref.shape[0]
    rb = 256

    @pl.when(pl.program_id(0) == 0)
    def _():
        col = lax.broadcasted_iota(jnp.int32, (rb, seq), 1)
        row = lax.broadcasted_iota(jnp.int32, (rb, seq), 0)

        def fill(i, carry):
            r0 = pl.multiple_of(i * rb, rb)
            tri_ref[pl.ds(r0, rb), :] = jnp.where(row + r0 < col, 1.0, 0.0).astype(BF16)
            return carry

        lax.fori_loop(0, seq // rb, fill, 0)

    aT = aff_ref[...].T[:n_experts, :]
    def search(i, prefix):
        cand = prefix | (1 << (30 - i))
        cnt = jnp.sum(jnp.where(aT >= pltpu.bitcast(cand, F32), 1.0, 0.0), axis=1, keepdims=True)
        return jnp.where(cnt >= cap, cand, prefix)

    thr_bits = lax.fori_loop(0, 31, search, jnp.zeros((n_experts, 1), jnp.int32))
    thr = pltpu.bitcast(thr_bits, F32)
    gt = aT > thr
    eq = aT == thr
    n_gt = jnp.sum(jnp.where(gt, 1.0, 0.0), axis=1, keepdims=True)
    tri = tri_ref[...]
    eq_rank = _dot(jnp.where(eq, 1.0, 0.0).astype(BF16), tri)
    sel = gt | (eq & (eq_rank < cap - n_gt))
    pos = _dot(jnp.where(sel, 1.0, 0.0).astype(BF16), tri)
    posm = jnp.where(sel, pos, -1.0)
    pos_ref[...] = posm
    aT_ref[...] = aT
    pad = jnp.full((LANES - n_experts, seq), -1.0, F32)
    posT_ref[...] = jnp.concatenate([posm, pad], axis=0).T

    hn = hn_ref[...]
    slot = lax.broadcasted_iota(jnp.int32, (cap, seq), 0).astype(F32)

    def gather(e, carry):
        hit = pos_ref[pl.ds(e, 1), :] == slot
        onehot = jnp.where(hit, 1.0, 0.0).astype(BF16)
        x_ref[e] = _dot(onehot, hn).astype(BF16)
        gate_ref[e] = jnp.sum(jnp.where(hit, aT_ref[pl.ds(e, 1), :], 0.0), axis=1, keepdims=True)
        return carry

    lax.fori_loop(0, n_experts, gather, 0)


def _route(aff, hn, batch, seq, n_experts, cap):
    d = hn.shape[1]
    return pl.pallas_call(
        functools.partial(_route_kernel, n_experts=n_experts, cap=cap),
        grid=(batch,),
        in_specs=[
            pl.BlockSpec((seq, LANES), lambda b: (b, 0)),
            pl.BlockSpec((seq, d), lambda b: (b, 0)),
        ],
        out_specs=[
            pl.BlockSpec((n_experts, cap, d), lambda b: (0, b, 0)),
            pl.BlockSpec((n_experts, cap, 1), lambda b: (0, b, 0)),
            pl.BlockSpec((seq, LANES), lambda b: (b, 0)),
        ],
        out_shape=[
            jax.ShapeDtypeStruct((n_experts, batch * cap, d), BF16),
            jax.ShapeDtypeStruct((n_experts, batch * cap, 1), F32),
            jax.ShapeDtypeStruct((batch * seq, LANES), F32),
        ],
        scratch_shapes=[
            pltpu.VMEM((seq, seq), BF16),
            pltpu.VMEM((n_experts, seq), F32),
            pltpu.VMEM((n_experts, seq), F32),
        ],
        compiler_params=_cparams(("arbitrary",)),
        name="route",
    )(aff, hn)


def _ffn_kernel(x_ref, gate_ref, wg_ref, wu_ref, wd_ref, y_ref, acc_ref, *, chunk):
    f = pl.program_id(2)
    wg = wg_ref[...].astype(BF16)
    wu = wu_ref[...].astype(BF16)
    wd = wd_ref[...].astype(BF16)

    @pl.when(f == 0)
    def _():
        acc_ref[...] = jnp.zeros_like(acc_ref)

    def hidden(c):
        x = x_ref[c * chunk:(c + 1) * chunk, :]
        g = _dot(x, wg)
        u = _dot(x, wu)
        return (g * (1.0 / (1.0 + jnp.exp(-g))) * u).astype(BF16)

    def project(c, a):
        acc_ref[c * chunk:(c + 1) * chunk, :] += _dot(a, wd)

    _software_pipeline(x_ref.shape[0] // chunk, hidden, project)

    @pl.when(f == pl.num_programs(2) - 1)
    def _():
        y_ref[...] = (acc_ref[...] * gate_ref[...]).astype(BF16)


def _ffn(xg, gate, w_gate, w_up, w_down, tm, tf):
    e, m, d = xg.shape
    dff = w_gate.shape[2]
    return pl.pallas_call(
        functools.partial(_ffn_kernel, chunk=512),
        grid=(e, m // tm, dff // tf),
        in_specs=[
            pl.BlockSpec((None, tm, d), lambda e, i, f: (e, i, 0)),
            pl.BlockSpec((None, tm, 1), lambda e, i, f: (e, i, 0)),
            pl.BlockSpec((None, d, tf), lambda e, i, f: (e, 0, f)),
            pl.BlockSpec((None, d, tf), lambda e, i, f: (e, 0, f)),
            pl.BlockSpec((None, tf, d), lambda e, i, f: (e, f, 0)),
        ],
        out_specs=pl.BlockSpec((None, tm, d), lambda e, i, f: (e, i, 0)),
        out_shape=jax.ShapeDtypeStruct((e, m, d), BF16),
        scratch_shapes=[pltpu.VMEM((tm, d), F32)],
        compiler_params=_cparams(("parallel", "parallel", "arbitrary")),
        name="ffn",
    )(xg, gate, w_gate, w_up, w_down)


def _combine_kernel(h_ref, posT_ref, y_ref, o_ref, *, n_experts, cap):
    tt = h_ref.shape[0]
    post = posT_ref[...]
    slot = lax.broadcasted_iota(jnp.int32, (tt, cap), 1).astype(F32)
    parts = []
    for e in range(n_experts):
        hit = jnp.broadcast_to(post[:, e:e + 1], (tt, cap)) == slot
        parts.append(jnp.where(hit, 1.0, 0.0).astype(BF16))
    onehot = jnp.concatenate(parts, axis=1)
    d = y_ref.shape[2]
    y = y_ref[...].reshape(n_experts * cap, d)
    o_ref[...] = h_ref[...] + _dot(onehot, y)


def _combine(h, posT, y, batch, seq, n_experts, cap, tt):
    n, d = h.shape
    per_seq = seq // tt
    return pl.pallas_call(
        functools.partial(_combine_kernel, n_experts=n_experts, cap=cap),
        grid=(batch, per_seq),
        in_specs=[
            pl.BlockSpec((tt, d), lambda b, i: (b * per_seq + i, 0)),
            pl.BlockSpec((tt, LANES), lambda b, i: (b * per_seq + i, 0)),
            pl.BlockSpec((n_experts, cap, d), lambda b, i: (0, b, 0)),
        ],
        out_specs=pl.BlockSpec((tt, d), lambda b, i: (b * per_seq + i, 0)),
        out_shape=jax.ShapeDtypeStruct((n, d), F32),
        compiler_params=_cparams(("parallel", "arbitrary")),
        name="combine",
    )(h, posT, y)


def _rope_tables(seq):
    t = np.arange(seq)
    row = (t // GRID_W).astype(np.float32)
    col = (t % GRID_W).astype(np.float32)
    half = HEAD_DIM // 2
    inv_freq = 1.0 / (ROPE_THETA ** (jnp.arange(0, half, 2, dtype=F32) / half))
    ang = jnp.concatenate([row[:, None] * inv_freq[None], col[:, None] * inv_freq[None]], axis=-1)
    cos, sin = jnp.cos(ang), jnp.sin(ang)
    lane = np.arange(LANES)
    pair = (lane % HEAD_DIM) // 2
    even = (lane % 2 == 0)[None, :]
    cos_l = cos[:, pair]
    sin_l = sin[:, pair]
    return cos_l, jnp.where(even, -sin_l, 0.0), jnp.where(even, 0.0, sin_l)


def _layer(x2, batch, seq, norm1_g, w_in, na_q_g, na_k_g, na_rpb, gqa_q_g, gqa_k_g,
           na_out_g, gqa_out_g, w_out, norm2_g, w_router, w_gate, w_up, w_down):
    d = x2.shape[1]
    n_experts = w_router.shape[1]
    cap = EC_CAPACITY_FACTOR * seq // n_experts
    rows = seq // GRID_W

    per_kv = GQA_HEADS // GQA_KV_HEADS
    head_order = np.arange(GQA_HEADS).reshape(GQA_KV_HEADS, per_kv).T.reshape(-1)
    perm = (head_order[:, None] * HEAD_DIM + np.arange(HEAD_DIM)[None, :]).reshape(-1)
    qb0 = 3 * NA_WIDTH
    w_in_p = jnp.concatenate([w_in[:, :qb0], w_in[:, qb0 + perm], w_in[:, qb0 + GQA_WIDTH:]], axis=1)
    w_out_p = jnp.concatenate([w_out[:NA_WIDTH], w_out[NA_WIDTH + perm]], axis=0)
    gqa_out_g_p = gqa_out_g[perm]

    hd = np.arange(max(NA_WIDTH, GQA_WIDTH)) // HEAD_DIM
    bd = jnp.asarray(hd[:, None] == hd[None, :], dtype=BF16)
    cos_t, sn_t, sp_t = _rope_tables(seq)
    tile_g = lambda g, heads: jnp.tile(g, heads)[None, :]

    qa, ka, va, qb, kb, vb = _inproj(
        x2, norm1_g[None, :], w_in_p.astype(BF16), bd,
        tile_g(na_q_g, NA_HEADS), tile_g(na_k_g, NA_HEADS),
        tile_g(gqa_q_g, GQA_HEADS), tile_g(gqa_k_g, GQA_KV_HEADS),
        cos_t, sn_t, sp_t, seq, tm=512)

    bias = _na_bias_table(na_rpb, rows)
    ma = _na_attn(qa, ka, va, bias, na_out_g[None, :], batch, seq)
    mb = _gqa_attn(qb, kb, vb, gqa_out_g_p[None, :], batch, seq, tq=256)

    wr = jnp.zeros((d, LANES), F32).at[:, :n_experts].set(w_router).astype(BF16)
    h, hn, aff = _post(x2, ma, mb, w_out_p.astype(BF16), norm2_g[None, :], wr, n_experts, tm=512)

    xg, gate, posT = _route(aff, hn, batch, seq, n_experts, cap)
    y = _ffn(xg, gate, w_gate, w_up, w_down, tm=2048, tf=256)
    return _combine(h, posT, y, batch, seq, n_experts, cap, tt=512)


def kernel(x, norm1_g, w_in, na_q_norm_g, na_k_norm_g, na_rpb, gqa_q_norm_g, gqa_k_norm_g,
           na_out_g, gqa_out_g, w_out, norm2_g, w_router, w_gate, w_up, w_down):
    batch, seq, d = x.shape
    x2 = x.reshape(batch * seq, d)
    for l in range(norm1_g.shape[0]):
        x2 = _layer(x2, batch, seq, norm1_g[l], w_in[l], na_q_norm_g[l], na_k_norm_g[l], na_rpb[l],
                    gqa_q_norm_g[l], gqa_k_norm_g[l], na_out_g[l], gqa_out_g[l], w_out[l],
                    norm2_g[l], w_router[l], w_gate[l], w_up[l], w_down[l])
    return x2.reshape(batch, seq, d)
```

```python
import functools

import jax
import jax.numpy as jnp
import numpy as np
from jax import lax
from jax.experimental import pallas as pl
from jax.experimental.pallas import tpu as pltpu

F32 = jnp.float32
BF16 = jnp.bfloat16

GRID_W = 64
HEAD_DIM = 64
NA_HEADS = 8
NA_WIN_H = 8
NA_WIN_W = 16
GQA_HEADS = 8
GQA_KV_HEADS = 2
ROPE_THETA = 10000.0
N_EXPERTS = 16
EC_CAPACITY_FACTOR = 2
NORM_EPS = 1e-6

LANES = 128
MXU_TILE = 256
NA_WIDTH = NA_HEADS * HEAD_DIM
GQA_WIDTH = GQA_HEADS * HEAD_DIM
GQA_KV_WIDTH = GQA_KV_HEADS * HEAD_DIM
NA_QROWS = 2
NA_KROWS = NA_WIN_H + NA_QROWS
MASK_NEG = -1e30
LOG2E = 1.4426950408889634
GQA_SCORE_LIMIT = 32.0

VMEM_LIMIT = 56 * 1024 * 1024


def _cparams(sem):
    return pltpu.CompilerParams(dimension_semantics=sem, vmem_limit_bytes=VMEM_LIMIT)


def _dot(a, b):
    return jnp.dot(a, b, preferred_element_type=F32)


def _dot_nt(a, b):
    return lax.dot_general(a, b, (((1,), (1,)), ((), ())), preferred_element_type=F32)


def _dot_tn(a, b):
    return lax.dot_general(a, b, (((0,), (0,)), ((), ())), preferred_element_type=F32)


def _software_pipeline(n, first, second, depth=1):
    pending = [first(i) for i in range(min(depth, n))]
    outs = []
    for i in range(n):
        if i + depth < n:
            pending.append(first(i + depth))
        outs.append(second(i, pending.pop(0)))
    return outs


def _rms(x, g):
    ms = jnp.mean(x * x, axis=-1, keepdims=True)
    return x * lax.rsqrt(ms + NORM_EPS) * g


def _inproj_kernel(x_ref, g1_ref, w_ref, bd_ref, gq_a_ref, gk_a_ref, gq_b_ref, gk_b_ref,
                   cos_ref, sn_ref, sp_ref,
                   qa_ref, ka_ref, va_ref, qb_ref, kb_ref, vb_ref):
    u = _rms(x_ref[...], g1_ref[...]).astype(BF16)
    proj = _dot(u, w_ref[...])

    def head_norm(t, g_row):
        w = t.shape[1]
        cw = min(w, bd_ref.shape[0])
        sq = t * t
        hi = sq.astype(BF16)
        lo = (sq - hi.astype(F32)).astype(BF16)
        bd = bd_ref[:cw, :cw]
        parts = [_dot(hi[:, j:j + cw], bd) + _dot(lo[:, j:j + cw], bd) for j in range(0, w, cw)]
        ss = parts[0] if len(parts) == 1 else jnp.concatenate(parts, axis=1)
        return t * lax.rsqrt(ss * (1.0 / HEAD_DIM) + NORM_EPS) * g_row

    def rope(t):
        c, sn, sp = cos_ref[...], sn_ref[...], sp_ref[...]
        outs = []
        for j in range(t.shape[1] // LANES):
            xc = t[:, j * LANES:(j + 1) * LANES]
            nxt = pltpu.roll(xc, LANES - 1, 1)
            prv = pltpu.roll(xc, 1, 1)
            outs.append(xc * c + nxt * sn + prv * sp)
        return outs[0] if len(outs) == 1 else jnp.concatenate(outs, axis=1)

    scale = HEAD_DIM ** -0.5 * LOG2E
    o = 0
    qa = proj[:, o:o + NA_WIDTH]; o += NA_WIDTH
    ka = proj[:, o:o + NA_WIDTH]; o += NA_WIDTH
    va = proj[:, o:o + NA_WIDTH]; o += NA_WIDTH
    qb = proj[:, o:o + GQA_WIDTH]; o += GQA_WIDTH
    kb = proj[:, o:o + GQA_KV_WIDTH]; o += GQA_KV_WIDTH
    vb = proj[:, o:o + GQA_KV_WIDTH]

    qa_ref[...] = (head_norm(qa, gq_a_ref[...]) * scale).astype(BF16)
    ka_ref[...] = head_norm(ka, gk_a_ref[...]).astype(BF16)
    va_ref[...] = va.astype(BF16)
    qb_ref[...] = (rope(head_norm(qb, gq_b_ref[...])) * scale).astype(BF16)
    kb_ref[...] = rope(head_norm(kb, gk_b_ref[...])).astype(BF16)
    vb_ref[...] = vb.astype(BF16)


def _inproj(x2, g1, w_bf, bd, gq_a, gk_a, gq_b, gk_b, cos_t, sn_t, sp_t, seq, tm):
    n, d = x2.shape
    in_w = w_bf.shape[1]
    per_seq = seq // tm
    row = lambda i: (i, 0)
    const = lambda i: (0, 0)
    pos = lambda i: (i % per_seq, 0)
    outs = [jax.ShapeDtypeStruct((n, w), BF16)
            for w in (NA_WIDTH, NA_WIDTH, NA_WIDTH, GQA_WIDTH, GQA_KV_WIDTH, GQA_KV_WIDTH)]
    return pl.pallas_call(
        _inproj_kernel,
        grid=(n // tm,),
        in_specs=[
            pl.BlockSpec((tm, d), row),
            pl.BlockSpec((1, d), const),
            pl.BlockSpec((d, in_w), const),
            pl.BlockSpec(bd.shape, const),
            pl.BlockSpec((1, NA_WIDTH), const),
            pl.BlockSpec((1, NA_WIDTH), const),
            pl.BlockSpec((1, GQA_WIDTH), const),
            pl.BlockSpec((1, GQA_KV_WIDTH), const),
            pl.BlockSpec((tm, LANES), pos),
            pl.BlockSpec((tm, LANES), pos),
            pl.BlockSpec((tm, LANES), pos),
        ],
        out_specs=[pl.BlockSpec((tm, s.shape[1]), row) for s in outs],
        out_shape=outs,
        compiler_params=_cparams(("parallel",)),
        name="inproj",
    )(x2, g1, w_bf, bd, gq_a, gk_a, gq_b, gk_b, cos_t, sn_t, sp_t)


def _na_kernel(q_ref, k_ref, v_ref, bias_ref, g_ref, o_ref, *, rows):
    j = pl.program_id(1)
    nq = NA_QROWS * GRID_W
    nk = NA_KROWS * GRID_W
    krow0 = jnp.clip(NA_QROWS * j - NA_WIN_H // 2, 0, rows - NA_KROWS)
    k0 = pl.multiple_of(krow0 * GRID_W, GRID_W)
    nblk = rows // NA_QROWS
    pat = jnp.clip(j, 0, 2) + jnp.clip(j - (nblk - 3), 0, 2)
    low = lax.broadcasted_iota(jnp.int32, (nq, LANES), 1) < HEAD_DIM
    top = lax.broadcasted_iota(jnp.int32, (LANES, nq), 0) < HEAD_DIM

    def scores(c):
        cs = slice(c * LANES, (c + 1) * LANES)
        kc = k_ref[pl.ds(k0, nk), cs]
        qc = q_ref[:, cs]
        zero = jnp.zeros_like(qc)
        qbd = jnp.concatenate([jnp.where(low, qc, zero), jnp.where(low, zero, qc)], axis=0)
        return _dot_nt(kc, qbd) + bias_ref[pat, c]

    def attend(c, st):
        m = jnp.max(st, axis=0, keepdims=True)
        p = jnp.exp2(st - m)
        l = jnp.sum(p, axis=0, keepdims=True)
        vc = v_ref[pl.ds(k0, nk), c * LANES:(c + 1) * LANES]
        ot = _dot_tn(vc, p.astype(BF16)) / l
        return jnp.where(top, ot[:, :nq], ot[:, nq:]).T

    o = jnp.concatenate(_software_pipeline(NA_WIDTH // LANES, scores, attend), axis=1)
    o_ref[...] = _rms(o, g_ref[...]).astype(BF16)


def _na_attn(qa, ka, va, bias, g, batch, seq):
    rows = seq // GRID_W
    nq = NA_QROWS * GRID_W
    nblk = rows // NA_QROWS
    return pl.pallas_call(
        functools.partial(_na_kernel, rows=rows),
        grid=(batch, nblk),
        in_specs=[
            pl.BlockSpec((nq, NA_WIDTH), lambda b, j: (b * nblk + j, 0)),
            pl.BlockSpec((seq, NA_WIDTH), lambda b, j: (b, 0)),
            pl.BlockSpec((seq, NA_WIDTH), lambda b, j: (b, 0)),
            pl.BlockSpec(bias.shape, lambda b, j: (0, 0, 0, 0)),
            pl.BlockSpec((1, NA_WIDTH), lambda b, j: (0, 0)),
        ],
        out_specs=pl.BlockSpec((nq, NA_WIDTH), lambda b, j: (b * nblk + j, 0)),
        out_shape=jax.ShapeDtypeStruct(qa.shape, BF16),
        compiler_params=_cparams(("parallel", "arbitrary")),
        name="na_attn",
    )(qa, ka, va, bias, g)


def _na_bias_table(rpb, rows):
    nblk = rows // NA_QROWS
    pats = [0, 1, 2, nblk - 2, nblk - 1]
    kh, kw = NA_WIN_H, NA_WIN_W
    nh, nrel_r, nrel_c = rpb.shape
    kc = np.arange(GRID_W)[:, None]
    qc = np.arange(GRID_W)[None, :]
    cs = np.clip(qc - kw // 2, 0, GRID_W - kw)
    col_ok = (kc >= cs) & (kc < cs + kw)
    ci = np.clip(kc - qc + (kw - 1), 0, nrel_c - 1)
    pick = (ci[None] == np.arange(nrel_c)[:, None, None]) & col_ok[None]
    toep = jnp.einsum('hdj,jkq->hdkq', rpb, jnp.asarray(pick, F32), precision=lax.Precision.HIGHEST)
    toep = jnp.where(col_ok[None, None], toep * LOG2E, MASK_NEG)
    masked = jnp.full((nh, GRID_W, GRID_W), MASK_NEG, F32)
    pat_blocks = []
    for j in pats:
        krow0 = int(np.clip(NA_QROWS * j - kh // 2, 0, rows - NA_KROWS))
        key_rows = []
        for kr in range(krow0, krow0 + NA_KROWS):
            per_q = []
            for qr in range(NA_QROWS * j, NA_QROWS * (j + 1)):
                rs = int(np.clip(qr - kh // 2, 0, rows - kh))
                per_q.append(toep[:, kr - qr + kh - 1] if rs <= kr < rs + kh else masked)
            key_rows.append(jnp.stack(per_q, axis=2))
        pat_blocks.append(jnp.stack(key_rows, axis=1))
    dense = jnp.stack(pat_blocks, axis=0)
    p = len(pats)
    nk, nq = NA_KROWS * GRID_W, NA_QROWS * GRID_W
    dense = dense.reshape(p, nh // 2, 2, nk, nq).transpose(0, 1, 3, 2, 4)
    return dense.reshape(p, nh // 2, nk, 2 * nq)


def _gqa_kernel(q_ref, k_ref, v_ref, g_ref, o_ref, *, tq, bounded_scores):
    k = k_ref[...]
    v = v_ref[...]
    kv_low = lax.broadcasted_iota(jnp.int32, v.shape, 1) < HEAD_DIM
    one = jnp.ones_like(v)
    v_aug = (jnp.where(kv_low, v, one), jnp.where(kv_low, one, v))
    low = lax.broadcasted_iota(jnp.int32, (tq, LANES), 1) < HEAD_DIM
    top = lax.broadcasted_iota(jnp.int32, (LANES, tq), 0) < HEAD_DIM
    nch = GQA_WIDTH // LANES

    def scores(i):
        c, grp = divmod(i, GQA_KV_HEADS)
        qc = q_ref[:, c * LANES:(c + 1) * LANES]
        keep = low if grp == 0 else jnp.logical_not(low)
        return _dot_nt(k, jnp.where(keep, qc, jnp.zeros_like(qc)))

    def attend(i, st):
        grp = i % GQA_KV_HEADS
        if not bounded_scores:
            st = st - jnp.max(st, axis=0, keepdims=True)
        p = jnp.exp2(st).astype(BF16)
        return _dot_tn(v_aug[grp], p)

    ots = _software_pipeline(nch * GQA_KV_HEADS, scores, attend, depth=2)
    outs = []
    for c in range(nch):
        o0, o1 = ots[GQA_KV_HEADS * c], ots[GQA_KV_HEADS * c + 1]
        l0 = o0[HEAD_DIM:HEAD_DIM + 1, :]
        l1 = o1[0:1, :]
        outs.append(jnp.where(top, o0 / l0, o1 / l1).T)
    o = jnp.concatenate(outs, axis=1)
    o_ref[...] = _rms(o, g_ref[...]).astype(BF16)


def _gqa_attn(qb, kb, vb, g, batch, seq, tq, bounded_scores):
    nblk = seq // tq
    return pl.pallas_call(
        functools.partial(_gqa_kernel, tq=tq, bounded_scores=bounded_scores),
        grid=(batch, nblk),
        in_specs=[
            pl.BlockSpec((tq, GQA_WIDTH), lambda b, i: (b * nblk + i, 0)),
            pl.BlockSpec((seq, GQA_KV_WIDTH), lambda b, i: (b, 0)),
            pl.BlockSpec((seq, GQA_KV_WIDTH), lambda b, i: (b, 0)),
            pl.BlockSpec((1, GQA_WIDTH), lambda b, i: (0, 0)),
        ],
        out_specs=pl.BlockSpec((tq, GQA_WIDTH), lambda b, i: (b * nblk + i, 0)),
        out_shape=jax.ShapeDtypeStruct(qb.shape, BF16),
        compiler_params=_cparams(("parallel", "arbitrary")),
        name="gqa_attn",
    )(qb, kb, vb, g)


def _post_kernel(x_ref, ma_ref, mb_ref, wo_ref, g2_ref, wr_ref, h_ref, hn_ref, aff_ref, *, n_experts):
    attn = _dot(ma_ref[...], wo_ref[:NA_WIDTH, :]) + _dot(mb_ref[...], wo_ref[NA_WIDTH:, :])
    h = x_ref[...] + attn
    h_ref[...] = h
    hn = _rms(h, g2_ref[...]).astype(BF16)
    hn_ref[...] = hn
    logits = _dot(hn, wr_ref[...])
    lane = lax.broadcasted_iota(jnp.int32, logits.shape, 1)
    logits = jnp.where(lane < n_experts, logits, MASK_NEG)
    m = jnp.max(logits, axis=-1, keepdims=True)
    e = jnp.exp(logits - m)
    aff_ref[...] = e / jnp.sum(e, axis=-1, keepdims=True)


def _post(x2, ma, mb, wo_bf, g2, wr_bf, n_experts, tm):
    n, d = x2.shape
    row = lambda i: (i, 0)
    const = lambda i: (0, 0)
    return pl.pallas_call(
        functools.partial(_post_kernel, n_experts=n_experts),
        grid=(n // tm,),
        in_specs=[
            pl.BlockSpec((tm, d), row),
            pl.BlockSpec((tm, NA_WIDTH), row),
            pl.BlockSpec((tm, GQA_WIDTH), row),
            pl.BlockSpec(wo_bf.shape, const),
            pl.BlockSpec((1, d), const),
            pl.BlockSpec(wr_bf.shape, const),
        ],
        out_specs=[pl.BlockSpec((tm, d), row), pl.BlockSpec((tm, d), row), pl.BlockSpec((tm, LANES), row)],
        out_shape=[jax.ShapeDtypeStruct((n, d), F32), jax.ShapeDtypeStruct((n, d), BF16),
                   jax.ShapeDtypeStruct((n, LANES), F32)],
        compiler_params=_cparams(("parallel",)),
        name="post",
    )(x2, ma, mb, wo_bf, g2, wr_bf)


def _route_kernel(aff_ref, hn_ref, x_ref, gate_ref, posT_ref, tri_ref, aT_ref, pos_ref, *, n_experts, cap):
    seq = aff_ref.shape[0]
    rb = 256

    @pl.when(pl.program_id(0) == 0)
    def _():
        col = lax.broadcasted_iota(jnp.int32, (rb, seq), 1)
        row = lax.broadcasted_iota(jnp.int32, (rb, seq), 0)

        def fill(i, carry):
            r0 = pl.multiple_of(i * rb, rb)
            tri_ref[pl.ds(r0, rb), :] = jnp.where(row + r0 < col, 1.0, 0.0).astype(BF16)
            return carry

        lax.fori_loop(0, seq // rb, fill, 0)

    aT = aff_ref[...].T[:n_experts, :]
    def search(i, prefix):
        cand = prefix | (1 << (30 - i))
        cnt = jnp.sum(jnp.where(aT >= pltpu.bitcast(cand, F32), 1.0, 0.0), axis=1, keepdims=True)
        return jnp.where(cnt >= cap, cand, prefix)

    thr_bits = lax.fori_loop(0, 31, search, jnp.zeros((n_experts, 1), jnp.int32))
    thr = pltpu.bitcast(thr_bits, F32)
    gt = aT > thr
    eq = aT == thr
    n_gt = jnp.sum(jnp.where(gt, 1.0, 0.0), axis=1, keepdims=True)
    tri = tri_ref[...]
    eq_rank = _dot(jnp.where(eq, 1.0, 0.0).astype(BF16), tri)
    sel = gt | (eq & (eq_rank < cap - n_gt))
    pos = _dot(jnp.where(sel, 1.0, 0.0).astype(BF16), tri)
    posm = jnp.where(sel, pos, -1.0)
    pos_ref[...] = posm
    aT_ref[...] = aT
    pad = jnp.full((LANES - n_experts, seq), -1.0, F32)
    posT_ref[...] = jnp.concatenate([posm, pad], axis=0).T

    hn = hn_ref[...]
    slot = lax.broadcasted_iota(jnp.int32, (cap, seq), 0).astype(F32)

    def gather(e, carry):
        hit = pos_ref[pl.ds(e, 1), :] == slot
        onehot = jnp.where(hit, 1.0, 0.0).astype(BF16)
        x_ref[e] = _dot(onehot, hn).astype(BF16)
        gate_ref[e] = jnp.sum(jnp.where(hit, aT_ref[pl.ds(e, 1), :], 0.0), axis=1, keepdims=True)
        return carry

    lax.fori_loop(0, n_experts, gather, 0)


def _route(aff, hn, batch, seq, n_experts, cap):
    d = hn.shape[1]
    return pl.pallas_call(
        functools.partial(_route_kernel, n_experts=n_experts, cap=cap),
        grid=(batch,),
        in_specs=[
            pl.BlockSpec((seq, LANES), lambda b: (b, 0)),
            pl.BlockSpec((seq, d), lambda b: (b, 0)),
        ],
        out_specs=[
            pl.BlockSpec((n_experts, cap, d), lambda b: (0, b, 0)),
            pl.BlockSpec((n_experts, cap, 1), lambda b: (0, b, 0)),
            pl.BlockSpec((seq, LANES), lambda b: (b, 0)),
        ],
        out_shape=[
            jax.ShapeDtypeStruct((n_experts, batch * cap, d), BF16),
            jax.ShapeDtypeStruct((n_experts, batch * cap, 1), F32),
            jax.ShapeDtypeStruct((batch * seq, LANES), F32),
        ],
        scratch_shapes=[
            pltpu.VMEM((seq, seq), BF16),
            pltpu.VMEM((n_experts, seq), F32),
            pltpu.VMEM((n_experts, seq), F32),
        ],
        compiler_params=_cparams(("arbitrary",)),
        name="route",
    )(aff, hn)


def _ffn_kernel(x_ref, gate_ref, wg_ref, wu_ref, wd_ref, y_ref, acc_ref, *, chunk):
    f = pl.program_id(2)
    wg = wg_ref[...].astype(BF16)
    wu = wu_ref[...].astype(BF16)
    wd = wd_ref[...].astype(BF16)

    @pl.when(f == 0)
    def _():
        acc_ref[...] = jnp.zeros_like(acc_ref)

    def hidden(c):
        x = x_ref[c * chunk:(c + 1) * chunk, :]
        g = _dot(x, wg)
        u = _dot(x, wu)
        return (g * (1.0 / (1.0 + jnp.exp(-g))) * u).astype(BF16)

    def project(c, a):
        acc_ref[c * chunk:(c + 1) * chunk, :] += _dot(a, wd)

    _software_pipeline(x_ref.shape[0] // chunk, hidden, project)

    @pl.when(f == pl.num_programs(2) - 1)
    def _():
        y_ref[...] = (acc_ref[...] * gate_ref[...]).astype(BF16)


def _ffn(xg, gate, w_gate, w_up, w_down, tm, tf):
    e, m, d = xg.shape
    dff = w_gate.shape[2]
    return pl.pallas_call(
        functools.partial(_ffn_kernel, chunk=512),
        grid=(e, m // tm, dff // tf),
        in_specs=[
            pl.BlockSpec((None, tm, d), lambda e, i, f: (e, i, 0)),
            pl.BlockSpec((None, tm, 1), lambda e, i, f: (e, i, 0)),
            pl.BlockSpec((None, d, tf), lambda e, i, f: (e, 0, f)),
            pl.BlockSpec((None, d, tf), lambda e, i, f: (e, 0, f)),
            pl.BlockSpec((None, tf, d), lambda e, i, f: (e, f, 0)),
        ],
        out_specs=pl.BlockSpec((None, tm, d), lambda e, i, f: (e, i, 0)),
        out_shape=jax.ShapeDtypeStruct((e, m, d), BF16),
        scratch_shapes=[pltpu.VMEM((tm, d), F32)],
        compiler_params=_cparams(("parallel", "parallel", "arbitrary")),
        name="ffn",
    )(xg, gate, w_gate, w_up, w_down)


def _combine_kernel(h_ref, posT_ref, y_ref, o_ref, *, n_experts, cap):
    tt = h_ref.shape[0]
    post = posT_ref[...]
    slot = lax.broadcasted_iota(jnp.int32, (tt, cap), 1).astype(F32)
    parts = []
    for e in range(n_experts):
        hit = jnp.broadcast_to(post[:, e:e + 1], (tt, cap)) == slot
        parts.append(jnp.where(hit, 1.0, 0.0).astype(BF16))
    onehot = jnp.concatenate(parts, axis=1)
    d = y_ref.shape[2]
    y = y_ref[...].reshape(n_experts * cap, d)
    o_ref[...] = h_ref[...] + _dot(onehot, y)


def _combine(h, posT, y, batch, seq, n_experts, cap, tt):
    n, d = h.shape
    per_seq = seq // tt
    return pl.pallas_call(
        functools.partial(_combine_kernel, n_experts=n_experts, cap=cap),
        grid=(batch, per_seq),
        in_specs=[
            pl.BlockSpec((tt, d), lambda b, i: (b * per_seq + i, 0)),
            pl.BlockSpec((tt, LANES), lambda b, i: (b * per_seq + i, 0)),
            pl.BlockSpec((n_experts, cap, d), lambda b, i: (0, b, 0)),
        ],
        out_specs=pl.BlockSpec((tt, d), lambda b, i: (b * per_seq + i, 0)),
        out_shape=jax.ShapeDtypeStruct((n, d), F32),
        compiler_params=_cparams(("parallel", "arbitrary")),
        name="combine",
    )(h, posT, y)


def _rope_tables(seq):
    t = np.arange(seq)
    row = (t // GRID_W).astype(np.float32)
    col = (t % GRID_W).astype(np.float32)
    half = HEAD_DIM // 2
    inv_freq = 1.0 / (ROPE_THETA ** (jnp.arange(0, half, 2, dtype=F32) / half))
    ang = jnp.concatenate([row[:, None] * inv_freq[None], col[:, None] * inv_freq[None]], axis=-1)
    cos, sin = jnp.cos(ang), jnp.sin(ang)
    lane = np.arange(LANES)
    pair = (lane % HEAD_DIM) // 2
    even = (lane % 2 == 0)[None, :]
    cos_l = cos[:, pair]
    sin_l = sin[:, pair]
    return cos_l, jnp.where(even, -sin_l, 0.0), jnp.where(even, 0.0, sin_l)


def _layer(x2, batch, seq, norm1_g, w_in, na_q_g, na_k_g, na_rpb, gqa_q_g, gqa_k_g,
           na_out_g, gqa_out_g, w_out, norm2_g, w_router, w_gate, w_up, w_down):
    d = x2.shape[1]
    n_experts = w_router.shape[1]
    cap = EC_CAPACITY_FACTOR * seq // n_experts
    rows = seq // GRID_W

    per_kv = GQA_HEADS // GQA_KV_HEADS
    head_order = np.arange(GQA_HEADS).reshape(GQA_KV_HEADS, per_kv).T.reshape(-1)
    perm = (head_order[:, None] * HEAD_DIM + np.arange(HEAD_DIM)[None, :]).reshape(-1)
    qb0 = 3 * NA_WIDTH
    w_in_p = jnp.concatenate([w_in[:, :qb0], w_in[:, qb0 + perm], w_in[:, qb0 + GQA_WIDTH:]], axis=1)
    w_out_p = jnp.concatenate([w_out[:NA_WIDTH], w_out[NA_WIDTH + perm]], axis=0)
    gqa_out_g_p = gqa_out_g[perm]

    hd = np.arange(MXU_TILE) // HEAD_DIM
    bd = jnp.asarray(hd[:, None] == hd[None, :], dtype=BF16)
    cos_t, sn_t, sp_t = _rope_tables(seq)
    tile_g = lambda g, heads: jnp.tile(g, heads)[None, :]

    qa, ka, va, qb, kb, vb = _inproj(
        x2, norm1_g[None, :], w_in_p.astype(BF16), bd,
        tile_g(na_q_g, NA_HEADS), tile_g(na_k_g, NA_HEADS),
        tile_g(gqa_q_g, GQA_HEADS), tile_g(gqa_k_g, GQA_KV_HEADS),
        cos_t, sn_t, sp_t, seq, tm=512)

    bias = _na_bias_table(na_rpb, rows)
    ma = _na_attn(qa, ka, va, bias, na_out_g[None, :], batch, seq)
    bound = 1.02 * HEAD_DIM ** 0.5 * LOG2E * jnp.max(jnp.abs(gqa_q_g)) * jnp.max(jnp.abs(gqa_k_g))
    gqa = lambda bounded: functools.partial(_gqa_attn, batch=batch, seq=seq, tq=256, bounded_scores=bounded)
    mb = lax.cond(bound <= GQA_SCORE_LIMIT, gqa(True), gqa(False), qb, kb, vb, gqa_out_g_p[None, :])

    wr = jnp.zeros((d, LANES), F32).at[:, :n_experts].set(w_router).astype(BF16)
    h, hn, aff = _post(x2, ma, mb, w_out_p.astype(BF16), norm2_g[None, :], wr, n_experts, tm=512)

    xg, gate, posT = _route(aff, hn, batch, seq, n_experts, cap)
    y = _ffn(xg, gate, w_gate, w_up, w_down, tm=2048, tf=256)
    return _combine(h, posT, y, batch, seq, n_experts, cap, tt=512)


def kernel(x, norm1_g, w_in, na_q_norm_g, na_k_norm_g, na_rpb, gqa_q_norm_g, gqa_k_norm_g,
           na_out_g, gqa_out_g, w_out, norm2_g, w_router, w_gate, w_up, w_down):
    batch, seq, d = x.shape
    x2 = x.reshape(batch * seq, d)
    for l in range(norm1_g.shape[0]):
        x2 = _layer(x2, batch, seq, norm1_g[l], w_in[l], na_q_norm_g[l], na_k_norm_g[l], na_rpb[l],
                    gqa_q_norm_g[l], gqa_k_norm_g[l], na_out_g[l], gqa_out_g[l], w_out[l],
                    norm2_g[l], w_router[l], w_gate[l], w_up[l], w_down[l])
    return x2.reshape(batch, seq, d)
```

```python
import functools

import jax
import jax.numpy as jnp
import numpy as np
from jax import lax
from jax.experimental import pallas as pl
from jax.experimental.pallas import tpu as pltpu

F32 = jnp.float32
BF16 = jnp.bfloat16

GRID_W = 64
HEAD_DIM = 64
NA_HEADS = 8
NA_WIN_H = 8
NA_WIN_W = 16
GQA_HEADS = 8
GQA_KV_HEADS = 2
ROPE_THETA = 10000.0
N_EXPERTS = 16
EC_CAPACITY_FACTOR = 2
NORM_EPS = 1e-6

LANES = 128
MXU_TILE = 256
NA_WIDTH = NA_HEADS * HEAD_DIM
GQA_WIDTH = GQA_HEADS * HEAD_DIM
GQA_KV_WIDTH = GQA_KV_HEADS * HEAD_DIM
NA_QROWS = 2
NA_KROWS = NA_WIN_H + NA_QROWS
NA_STEP_BLOCKS = 4
MASK_NEG = -1e30
LOG2E = 1.4426950408889634
GQA_SCORE_LIMIT = 32.0

VMEM_LIMIT = 56 * 1024 * 1024


def _cparams(sem):
    return pltpu.CompilerParams(dimension_semantics=sem, vmem_limit_bytes=VMEM_LIMIT)


def _dot(a, b):
    return jnp.dot(a, b, preferred_element_type=F32)


def _dot_nt(a, b):
    return lax.dot_general(a, b, (((1,), (1,)), ((), ())), preferred_element_type=F32)


def _dot_tn(a, b):
    return lax.dot_general(a, b, (((0,), (0,)), ((), ())), preferred_element_type=F32)


def _software_pipeline(n, first, second, depth=1):
    pending = [first(i) for i in range(min(depth, n))]
    outs = []
    for i in range(n):
        if i + depth < n:
            pending.append(first(i + depth))
        outs.append(second(i, pending.pop(0)))
    return outs


def _rms(x, g):
    ms = jnp.mean(x * x, axis=-1, keepdims=True)
    return x * lax.rsqrt(ms + NORM_EPS) * g


def _inproj_kernel(x_ref, g1_ref, w_ref, bd_ref, gq_a_ref, gk_a_ref, gq_b_ref, gk_b_ref,
                   cos_ref, sn_ref, sp_ref,
                   qa_ref, ka_ref, va_ref, qb_ref, kb_ref, vb_ref):
    u = _rms(x_ref[...], g1_ref[...]).astype(BF16)
    proj = _dot(u, w_ref[...])

    def head_norm(t, g_row):
        w = t.shape[1]
        cw = min(w, bd_ref.shape[0])
        sq = t * t
        hi = sq.astype(BF16)
        lo = (sq - hi.astype(F32)).astype(BF16)
        bd = bd_ref[:cw, :cw]
        parts = [_dot(hi[:, j:j + cw], bd) + _dot(lo[:, j:j + cw], bd) for j in range(0, w, cw)]
        ss = parts[0] if len(parts) == 1 else jnp.concatenate(parts, axis=1)
        return t * lax.rsqrt(ss * (1.0 / HEAD_DIM) + NORM_EPS) * g_row

    def rope(t):
        c, sn, sp = cos_ref[...], sn_ref[...], sp_ref[...]
        outs = []
        for j in range(t.shape[1] // LANES):
            xc = t[:, j * LANES:(j + 1) * LANES]
            nxt = pltpu.roll(xc, LANES - 1, 1)
            prv = pltpu.roll(xc, 1, 1)
            outs.append(xc * c + nxt * sn + prv * sp)
        return outs[0] if len(outs) == 1 else jnp.concatenate(outs, axis=1)

    scale = HEAD_DIM ** -0.5 * LOG2E
    o = 0
    qa = proj[:, o:o + NA_WIDTH]; o += NA_WIDTH
    ka = proj[:, o:o + NA_WIDTH]; o += NA_WIDTH
    va = proj[:, o:o + NA_WIDTH]; o += NA_WIDTH
    qb = proj[:, o:o + GQA_WIDTH]; o += GQA_WIDTH
    kb = proj[:, o:o + GQA_KV_WIDTH]; o += GQA_KV_WIDTH
    vb = proj[:, o:o + GQA_KV_WIDTH]

    qa_ref[...] = (head_norm(qa, gq_a_ref[...]) * scale).astype(BF16)
    ka_ref[...] = head_norm(ka, gk_a_ref[...]).astype(BF16)
    va_ref[...] = va.astype(BF16)
    qb_ref[...] = (rope(head_norm(qb, gq_b_ref[...])) * scale).astype(BF16)
    kb_ref[...] = rope(head_norm(kb, gk_b_ref[...])).astype(BF16)
    vb_ref[...] = vb.astype(BF16)


def _inproj(x2, g1, w_bf, bd, gq_a, gk_a, gq_b, gk_b, cos_t, sn_t, sp_t, seq, tm):
    n, d = x2.shape
    in_w = w_bf.shape[1]
    per_seq = seq // tm
    row = lambda i: (i, 0)
    const = lambda i: (0, 0)
    pos = lambda i: (i % per_seq, 0)
    outs = [jax.ShapeDtypeStruct((n, w), BF16)
            for w in (NA_WIDTH, NA_WIDTH, NA_WIDTH, GQA_WIDTH, GQA_KV_WIDTH, GQA_KV_WIDTH)]
    return pl.pallas_call(
        _inproj_kernel,
        grid=(n // tm,),
        in_specs=[
            pl.BlockSpec((tm, d), row),
            pl.BlockSpec((1, d), const),
            pl.BlockSpec((d, in_w), const),
            pl.BlockSpec(bd.shape, const),
            pl.BlockSpec((1, NA_WIDTH), const),
            pl.BlockSpec((1, NA_WIDTH), const),
            pl.BlockSpec((1, GQA_WIDTH), const),
            pl.BlockSpec((1, GQA_KV_WIDTH), const),
            pl.BlockSpec((tm, LANES), pos),
            pl.BlockSpec((tm, LANES), pos),
            pl.BlockSpec((tm, LANES), pos),
        ],
        out_specs=[pl.BlockSpec((tm, s.shape[1]), row) for s in outs],
        out_shape=outs,
        compiler_params=_cparams(("parallel",)),
        name="inproj",
    )(x2, g1, w_bf, bd, gq_a, gk_a, gq_b, gk_b, cos_t, sn_t, sp_t)


def _na_kernel(q_ref, k_ref, v_ref, bias_ref, g_ref, o_ref, *, rows):
    nq = NA_QROWS * GRID_W
    nk = NA_KROWS * GRID_W
    nblk = rows // NA_QROWS
    nch = NA_WIDTH // LANES
    low = lax.broadcasted_iota(jnp.int32, (nq, LANES), 1) < HEAD_DIM
    top = lax.broadcasted_iota(jnp.int32, (LANES, nq), 0) < HEAD_DIM

    def block(jj):
        j = NA_STEP_BLOCKS * pl.program_id(1) + jj
        krow0 = jnp.clip(NA_QROWS * j - NA_WIN_H // 2, 0, rows - NA_KROWS)
        k0 = pl.multiple_of(krow0 * GRID_W, GRID_W)
        pat = jnp.clip(j, 0, 2) + jnp.clip(j - (nblk - 3), 0, 2)
        return k0, pat

    blocks = [block(jj) for jj in range(NA_STEP_BLOCKS)]

    def scores(s):
        jj, c = divmod(s, nch)
        k0, pat = blocks[jj]
        cs = slice(c * LANES, (c + 1) * LANES)
        kc = k_ref[pl.ds(k0, nk), cs]
        qc = q_ref[jj * nq:(jj + 1) * nq, cs]
        zero = jnp.zeros_like(qc)
        qbd = jnp.concatenate([jnp.where(low, qc, zero), jnp.where(low, zero, qc)], axis=0)
        return _dot_nt(kc, qbd) + bias_ref[pat, c]

    def attend(s, st):
        jj, c = divmod(s, nch)
        k0, _ = blocks[jj]
        m = jnp.max(st, axis=0, keepdims=True)
        p = jnp.exp2(st - m)
        l = jnp.sum(p, axis=0, keepdims=True)
        vc = v_ref[pl.ds(k0, nk), c * LANES:(c + 1) * LANES]
        ot = _dot_tn(vc, p.astype(BF16)) / l
        return jnp.where(top, ot[:, :nq], ot[:, nq:]).T

    outs = _software_pipeline(NA_STEP_BLOCKS * nch, scores, attend)
    for jj in range(NA_STEP_BLOCKS):
        o = jnp.concatenate(outs[jj * nch:(jj + 1) * nch], axis=1)
        o_ref[jj * nq:(jj + 1) * nq, :] = _rms(o, g_ref[...]).astype(BF16)


def _na_attn(qa, ka, va, bias, g, batch, seq):
    rows = seq // GRID_W
    nq = NA_STEP_BLOCKS * NA_QROWS * GRID_W
    nblk = seq // nq
    return pl.pallas_call(
        functools.partial(_na_kernel, rows=rows),
        grid=(batch, nblk),
        in_specs=[
            pl.BlockSpec((nq, NA_WIDTH), lambda b, j: (b * nblk + j, 0)),
            pl.BlockSpec((seq, NA_WIDTH), lambda b, j: (b, 0)),
            pl.BlockSpec((seq, NA_WIDTH), lambda b, j: (b, 0)),
            pl.BlockSpec(bias.shape, lambda b, j: (0, 0, 0, 0)),
            pl.BlockSpec((1, NA_WIDTH), lambda b, j: (0, 0)),
        ],
        out_specs=pl.BlockSpec((nq, NA_WIDTH), lambda b, j: (b * nblk + j, 0)),
        out_shape=jax.ShapeDtypeStruct(qa.shape, BF16),
        compiler_params=_cparams(("parallel", "arbitrary")),
        name="na_attn",
    )(qa, ka, va, bias, g)


def _na_bias_table(rpb, rows):
    nblk = rows // NA_QROWS
    pats = [0, 1, 2, nblk - 2, nblk - 1]
    kh, kw = NA_WIN_H, NA_WIN_W
    nh, nrel_r, nrel_c = rpb.shape
    kc = np.arange(GRID_W)[:, None]
    qc = np.arange(GRID_W)[None, :]
    cs = np.clip(qc - kw // 2, 0, GRID_W - kw)
    col_ok = (kc >= cs) & (kc < cs + kw)
    ci = np.clip(kc - qc + (kw - 1), 0, nrel_c - 1)
    pick = (ci[None] == np.arange(nrel_c)[:, None, None]) & col_ok[None]
    toep = jnp.einsum('hdj,jkq->hdkq', rpb, jnp.asarray(pick, F32), precision=lax.Precision.HIGHEST)
    toep = jnp.where(col_ok[None, None], toep * LOG2E, MASK_NEG)
    masked = jnp.full((nh, GRID_W, GRID_W), MASK_NEG, F32)
    pat_blocks = []
    for j in pats:
        krow0 = int(np.clip(NA_QROWS * j - kh // 2, 0, rows - NA_KROWS))
        key_rows = []
        for kr in range(krow0, krow0 + NA_KROWS):
            per_q = []
            for qr in range(NA_QROWS * j, NA_QROWS * (j + 1)):
                rs = int(np.clip(qr - kh // 2, 0, rows - kh))
                per_q.append(toep[:, kr - qr + kh - 1] if rs <= kr < rs + kh else masked)
            key_rows.append(jnp.stack(per_q, axis=2))
        pat_blocks.append(jnp.stack(key_rows, axis=1))
    dense = jnp.stack(pat_blocks, axis=0)
    p = len(pats)
    nk, nq = NA_KROWS * GRID_W, NA_QROWS * GRID_W
    dense = dense.reshape(p, nh // 2, 2, nk, nq).transpose(0, 1, 3, 2, 4)
    return dense.reshape(p, nh // 2, nk, 2 * nq)


def _gqa_kernel(q_ref, k_ref, v_ref, g_ref, o_ref, *, tq, bounded_scores):
    k = k_ref[...]
    v = v_ref[...]
    kv_low = lax.broadcasted_iota(jnp.int32, v.shape, 1) < HEAD_DIM
    one = jnp.ones_like(v)
    v_aug = (jnp.where(kv_low, v, one), jnp.where(kv_low, one, v))
    low = lax.broadcasted_iota(jnp.int32, (tq, LANES), 1) < HEAD_DIM
    top = lax.broadcasted_iota(jnp.int32, (LANES, tq), 0) < HEAD_DIM
    nch = GQA_WIDTH // LANES

    def scores(i):
        c, grp = divmod(i, GQA_KV_HEADS)
        qc = q_ref[:, c * LANES:(c + 1) * LANES]
        keep = low if grp == 0 else jnp.logical_not(low)
        return _dot_nt(k, jnp.where(keep, qc, jnp.zeros_like(qc)))

    def attend(i, st):
        grp = i % GQA_KV_HEADS
        if not bounded_scores:
            st = st - jnp.max(st, axis=0, keepdims=True)
        p = jnp.exp2(st).astype(BF16)
        return _dot_tn(v_aug[grp], p)

    ots = _software_pipeline(nch * GQA_KV_HEADS, scores, attend, depth=2)
    outs = []
    for c in range(nch):
        o0, o1 = ots[GQA_KV_HEADS * c], ots[GQA_KV_HEADS * c + 1]
        l0 = o0[HEAD_DIM:HEAD_DIM + 1, :]
        l1 = o1[0:1, :]
        outs.append(jnp.where(top, o0 / l0, o1 / l1).T)
    o = jnp.concatenate(outs, axis=1)
    o_ref[...] = _rms(o, g_ref[...]).astype(BF16)


def _gqa_attn(qb, kb, vb, g, batch, seq, tq, bounded_scores):
    nblk = seq // tq
    return pl.pallas_call(
        functools.partial(_gqa_kernel, tq=tq, bounded_scores=bounded_scores),
        grid=(batch, nblk),
        in_specs=[
            pl.BlockSpec((tq, GQA_WIDTH), lambda b, i: (b * nblk + i, 0)),
            pl.BlockSpec((seq, GQA_KV_WIDTH), lambda b, i: (b, 0)),
            pl.BlockSpec((seq, GQA_KV_WIDTH), lambda b, i: (b, 0)),
            pl.BlockSpec((1, GQA_WIDTH), lambda b, i: (0, 0)),
        ],
        out_specs=pl.BlockSpec((tq, GQA_WIDTH), lambda b, i: (b * nblk + i, 0)),
        out_shape=jax.ShapeDtypeStruct(qb.shape, BF16),
        compiler_params=_cparams(("parallel", "arbitrary")),
        name="gqa_attn",
    )(qb, kb, vb, g)


def _post_kernel(x_ref, ma_ref, mb_ref, wo_ref, g2_ref, wr_ref, h_ref, hn_ref, aff_ref, *, n_experts):
    attn = _dot(ma_ref[...], wo_ref[:NA_WIDTH, :]) + _dot(mb_ref[...], wo_ref[NA_WIDTH:, :])
    h = x_ref[...] + attn
    h_ref[...] = h
    hn = _rms(h, g2_ref[...]).astype(BF16)
    hn_ref[...] = hn
    logits = _dot(hn, wr_ref[...])
    lane = lax.broadcasted_iota(jnp.int32, logits.shape, 1)
    logits = jnp.where(lane < n_experts, logits, MASK_NEG)
    m = jnp.max(logits, axis=-1, keepdims=True)
    e = jnp.exp(logits - m)
    aff_ref[...] = e / jnp.sum(e, axis=-1, keepdims=True)


def _post(x2, ma, mb, wo_bf, g2, wr_bf, n_experts, tm):
    n, d = x2.shape
    row = lambda i: (i, 0)
    const = lambda i: (0, 0)
    return pl.pallas_call(
        functools.partial(_post_kernel, n_experts=n_experts),
        grid=(n // tm,),
        in_specs=[
            pl.BlockSpec((tm, d), row),
            pl.BlockSpec((tm, NA_WIDTH), row),
            pl.BlockSpec((tm, GQA_WIDTH), row),
            pl.BlockSpec(wo_bf.shape, const),
            pl.BlockSpec((1, d), const),
            pl.BlockSpec(wr_bf.shape, const),
        ],
        out_specs=[pl.BlockSpec((tm, d), row), pl.BlockSpec((tm, d), row), pl.BlockSpec((tm, LANES), row)],
        out_shape=[jax.ShapeDtypeStruct((n, d), F32), jax.ShapeDtypeStruct((n, d), BF16),
                   jax.ShapeDtypeStruct((n, LANES), F32)],
        compiler_params=_cparams(("parallel",)),
        name="post",
    )(x2, ma, mb, wo_bf, g2, wr_bf)


def _route_kernel(aff_ref, hn_ref, x_ref, gate_ref, posT_ref, tri_ref, aT_ref, pos_ref, *, n_experts, cap):
    seq = aff_ref.shape[0]
    rb = 256

    @pl.when(pl.program_id(0) == 0)
    def _():
        col = lax.broadcasted_iota(jnp.int32, (rb, seq), 1)
        row = lax.broadcasted_iota(jnp.int32, (rb, seq), 0)

        def fill(i, carry):
            r0 = pl.multiple_of(i * rb, rb)
            tri_ref[pl.ds(r0, rb), :] = jnp.where(row + r0 < col, 1.0, 0.0).astype(BF16)
            return carry

        lax.fori_loop(0, seq // rb, fill, 0)

    aT = aff_ref[...].T[:n_experts, :]
    def search(i, prefix):
        cand = prefix | (1 << (30 - i))
        cnt = jnp.sum(jnp.where(aT >= pltpu.bitcast(cand, F32), 1.0, 0.0), axis=1, keepdims=True)
        return jnp.where(cnt >= cap, cand, prefix)

    thr_bits = lax.fori_loop(0, 31, search, jnp.zeros((n_experts, 1), jnp.int32))
    thr = pltpu.bitcast(thr_bits, F32)
    gt = aT > thr
    eq = aT == thr
    n_gt = jnp.sum(jnp.where(gt, 1.0, 0.0), axis=1, keepdims=True)
    tri = tri_ref[...]
    eq_rank = _dot(jnp.where(eq, 1.0, 0.0).astype(BF16), tri)
    sel = gt | (eq & (eq_rank < cap - n_gt))
    pos = _dot(jnp.where(sel, 1.0, 0.0).astype(BF16), tri)
    posm = jnp.where(sel, pos, -1.0)
    pos_ref[...] = posm
    aT_ref[...] = aT
    pad = jnp.full((LANES - n_experts, seq), -1.0, F32)
    posT_ref[...] = jnp.concatenate([posm, pad], axis=0).T

    hn = hn_ref[...]
    slot = lax.broadcasted_iota(jnp.int32, (cap, seq), 0).astype(F32)

    def gather(e, carry):
        hit = pos_ref[pl.ds(e, 1), :] == slot
        onehot = jnp.where(hit, 1.0, 0.0).astype(BF16)
        x_ref[e] = _dot(onehot, hn).astype(BF16)
        gate_ref[e] = jnp.sum(jnp.where(hit, aT_ref[pl.ds(e, 1), :], 0.0), axis=1, keepdims=True)
        return carry

    lax.fori_loop(0, n_experts, gather, 0)


def _route(aff, hn, batch, seq, n_experts, cap):
    d = hn.shape[1]
    return pl.pallas_call(
        functools.partial(_route_kernel, n_experts=n_experts, cap=cap),
        grid=(batch,),
        in_specs=[
            pl.BlockSpec((seq, LANES), lambda b: (b, 0)),
            pl.BlockSpec((seq, d), lambda b: (b, 0)),
        ],
        out_specs=[
            pl.BlockSpec((n_experts, cap, d), lambda b: (0, b, 0)),
            pl.BlockSpec((n_experts, cap, 1), lambda b: (0, b, 0)),
            pl.BlockSpec((seq, LANES), lambda b: (b, 0)),
        ],
        out_shape=[
            jax.ShapeDtypeStruct((n_experts, batch * cap, d), BF16),
            jax.ShapeDtypeStruct((n_experts, batch * cap, 1), F32),
            jax.ShapeDtypeStruct((batch * seq, LANES), F32),
        ],
        scratch_shapes=[
            pltpu.VMEM((seq, seq), BF16),
            pltpu.VMEM((n_experts, seq), F32),
            pltpu.VMEM((n_experts, seq), F32),
        ],
        compiler_params=_cparams(("arbitrary",)),
        name="route",
    )(aff, hn)


def _ffn_kernel(x_ref, gate_ref, wg_ref, wu_ref, wd_ref, y_ref, acc_ref, *, chunk):
    f = pl.program_id(2)
    wg = wg_ref[...].astype(BF16)
    wu = wu_ref[...].astype(BF16)
    wd = wd_ref[...].astype(BF16)

    @pl.when(f == 0)
    def _():
        acc_ref[...] = jnp.zeros_like(acc_ref)

    def hidden(c):
        x = x_ref[c * chunk:(c + 1) * chunk, :]
        g = _dot(x, wg)
        u = _dot(x, wu)
        return (g * (1.0 / (1.0 + jnp.exp(-g))) * u).astype(BF16)

    def project(c, a):
        acc_ref[c * chunk:(c + 1) * chunk, :] += _dot(a, wd)

    _software_pipeline(x_ref.shape[0] // chunk, hidden, project)

    @pl.when(f == pl.num_programs(2) - 1)
    def _():
        y_ref[...] = (acc_ref[...] * gate_ref[...]).astype(BF16)


def _ffn(xg, gate, w_gate, w_up, w_down, tm, tf):
    e, m, d = xg.shape
    dff = w_gate.shape[2]
    return pl.pallas_call(
        functools.partial(_ffn_kernel, chunk=512),
        grid=(e, m // tm, dff // tf),
        in_specs=[
            pl.BlockSpec((None, tm, d), lambda e, i, f: (e, i, 0)),
            pl.BlockSpec((None, tm, 1), lambda e, i, f: (e, i, 0)),
            pl.BlockSpec((None, d, tf), lambda e, i, f: (e, 0, f)),
            pl.BlockSpec((None, d, tf), lambda e, i, f: (e, 0, f)),
            pl.BlockSpec((None, tf, d), lambda e, i, f: (e, f, 0)),
        ],
        out_specs=pl.BlockSpec((None, tm, d), lambda e, i, f: (e, i, 0)),
        out_shape=jax.ShapeDtypeStruct((e, m, d), BF16),
        scratch_shapes=[pltpu.VMEM((tm, d), F32)],
        compiler_params=_cparams(("parallel", "parallel", "arbitrary")),
        name="ffn",
    )(xg, gate, w_gate, w_up, w_down)


def _combine_kernel(h_ref, posT_ref, y_ref, o_ref, *, n_experts, cap):
    tt = h_ref.shape[0]
    post = posT_ref[...]
    slot = lax.broadcasted_iota(jnp.int32, (tt, cap), 1).astype(F32)
    parts = []
    for e in range(n_experts):
        hit = jnp.broadcast_to(post[:, e:e + 1], (tt, cap)) == slot
        parts.append(jnp.where(hit, 1.0, 0.0).astype(BF16))
    onehot = jnp.concatenate(parts, axis=1)
    d = y_ref.shape[2]
    y = y_ref[...].reshape(n_experts * cap, d)
    o_ref[...] = h_ref[...] + _dot(onehot, y)


def _combine(h, posT, y, batch, seq, n_experts, cap, tt):
    n, d = h.shape
    per_seq = seq // tt
    return pl.pallas_call(
        functools.partial(_combine_kernel, n_experts=n_experts, cap=cap),
        grid=(batch, per_seq),
        in_specs=[
            pl.BlockSpec((tt, d), lambda b, i: (b * per_seq + i, 0)),
            pl.BlockSpec((tt, LANES), lambda b, i: (b * per_seq + i, 0)),
            pl.BlockSpec((n_experts, cap, d), lambda b, i: (0, b, 0)),
        ],
        out_specs=pl.BlockSpec((tt, d), lambda b, i: (b * per_seq + i, 0)),
        out_shape=jax.ShapeDtypeStruct((n, d), F32),
        compiler_params=_cparams(("parallel", "arbitrary")),
        name="combine",
    )(h, posT, y)


def _rope_tables(seq):
    t = np.arange(seq)
    row = (t // GRID_W).astype(np.float32)
    col = (t % GRID_W).astype(np.float32)
    half = HEAD_DIM // 2
    inv_freq = 1.0 / (ROPE_THETA ** (jnp.arange(0, half, 2, dtype=F32) / half))
    ang = jnp.concatenate([row[:, None] * inv_freq[None], col[:, None] * inv_freq[None]], axis=-1)
    cos, sin = jnp.cos(ang), jnp.sin(ang)
    lane = np.arange(LANES)
    pair = (lane % HEAD_DIM) // 2
    even = (lane % 2 == 0)[None, :]
    cos_l = cos[:, pair]
    sin_l = sin[:, pair]
    return cos_l, jnp.where(even, -sin_l, 0.0), jnp.where(even, 0.0, sin_l)


def _layer(x2, batch, seq, norm1_g, w_in, na_q_g, na_k_g, na_rpb, gqa_q_g, gqa_k_g,
           na_out_g, gqa_out_g, w_out, norm2_g, w_router, w_gate, w_up, w_down):
    d = x2.shape[1]
    n_experts = w_router.shape[1]
    cap = EC_CAPACITY_FACTOR * seq // n_experts
    rows = seq // GRID_W

    per_kv = GQA_HEADS // GQA_KV_HEADS
    head_order = np.arange(GQA_HEADS).reshape(GQA_KV_HEADS, per_kv).T.reshape(-1)
    perm = (head_order[:, None] * HEAD_DIM + np.arange(HEAD_DIM)[None, :]).reshape(-1)
    qb0 = 3 * NA_WIDTH
    w_in_p = jnp.concatenate([w_in[:, :qb0], w_in[:, qb0 + perm], w_in[:, qb0 + GQA_WIDTH:]], axis=1)
    w_out_p = jnp.concatenate([w_out[:NA_WIDTH], w_out[NA_WIDTH + perm]], axis=0)
    gqa_out_g_p = gqa_out_g[perm]

    hd = np.arange(MXU_TILE) // HEAD_DIM
    bd = jnp.asarray(hd[:, None] == hd[None, :], dtype=BF16)
    cos_t, sn_t, sp_t = _rope_tables(seq)
    tile_g = lambda g, heads: jnp.tile(g, heads)[None, :]

    qa, ka, va, qb, kb, vb = _inproj(
        x2, norm1_g[None, :], w_in_p.astype(BF16), bd,
        tile_g(na_q_g, NA_HEADS), tile_g(na_k_g, NA_HEADS),
        tile_g(gqa_q_g, GQA_HEADS), tile_g(gqa_k_g, GQA_KV_HEADS),
        cos_t, sn_t, sp_t, seq, tm=512)

    bias = _na_bias_table(na_rpb, rows)
    ma = _na_attn(qa, ka, va, bias, na_out_g[None, :], batch, seq)
    bound = 1.02 * HEAD_DIM ** 0.5 * LOG2E * jnp.max(jnp.abs(gqa_q_g)) * jnp.max(jnp.abs(gqa_k_g))
    gqa = lambda bounded: functools.partial(_gqa_attn, batch=batch, seq=seq, tq=512, bounded_scores=bounded)
    mb = lax.cond(bound <= GQA_SCORE_LIMIT, gqa(True), gqa(False), qb, kb, vb, gqa_out_g_p[None, :])

    wr = jnp.zeros((d, LANES), F32).at[:, :n_experts].set(w_router).astype(BF16)
    h, hn, aff = _post(x2, ma, mb, w_out_p.astype(BF16), norm2_g[None, :], wr, n_experts, tm=512)

    xg, gate, posT = _route(aff, hn, batch, seq, n_experts, cap)
    y = _ffn(xg, gate, w_gate, w_up, w_down, tm=2048, tf=256)
    return _combine(h, posT, y, batch, seq, n_experts, cap, tt=512)


def kernel(x, norm1_g, w_in, na_q_norm_g, na_k_norm_g, na_rpb, gqa_q_norm_g, gqa_k_norm_g,
           na_out_g, gqa_out_g, w_out, norm2_g, w_router, w_gate, w_up, w_down):
    batch, seq, d = x.shape
    x2 = x.reshape(batch * seq, d)
    for l in range(norm1_g.shape[0]):
        x2 = _layer(x2, batch, seq, norm1_g[l], w_in[l], na_q_norm_g[l], na_k_norm_g[l], na_rpb[l],
                    gqa_q_norm_g[l], gqa_k_norm_g[l], na_out_g[l], gqa_out_g[l], w_out[l],
                    norm2_g[l], w_router[l], w_gate[l], w_up[l], w_down[l])
    return x2.reshape(batch, seq, d)
```

```python
import functools

import jax
import jax.numpy as jnp
import numpy as np
from jax import lax
from jax.experimental import pallas as pl
from jax.experimental.pallas import tpu as pltpu

F32 = jnp.float32
BF16 = jnp.bfloat16

GRID_W = 64
HEAD_DIM = 64
NA_HEADS = 8
NA_WIN_H = 8
NA_WIN_W = 16
GQA_HEADS = 8
GQA_KV_HEADS = 2
ROPE_THETA = 10000.0
N_EXPERTS = 16
EC_CAPACITY_FACTOR = 2
NORM_EPS = 1e-6

LANES = 128
MXU_TILE = 256
NA_WIDTH = NA_HEADS * HEAD_DIM
GQA_WIDTH = GQA_HEADS * HEAD_DIM
GQA_KV_WIDTH = GQA_KV_HEADS * HEAD_DIM
NA_QROWS = 2
NA_KROWS = NA_WIN_H + NA_QROWS
NA_STEP_BLOCKS = 4
MASK_NEG = -1e30
LOG2E = 1.4426950408889634
GQA_SCORE_LIMIT = 32.0

VMEM_LIMIT = 56 * 1024 * 1024


def _cparams(sem):
    return pltpu.CompilerParams(dimension_semantics=sem, vmem_limit_bytes=VMEM_LIMIT)


def _dot(a, b):
    return jnp.dot(a, b, preferred_element_type=F32)


def _dot_nt(a, b):
    return lax.dot_general(a, b, (((1,), (1,)), ((), ())), preferred_element_type=F32)


def _dot_tn(a, b):
    return lax.dot_general(a, b, (((0,), (0,)), ((), ())), preferred_element_type=F32)


def _software_pipeline(n, first, second, depth=1):
    pending = [first(i) for i in range(min(depth, n))]
    outs = []
    for i in range(n):
        if i + depth < n:
            pending.append(first(i + depth))
        outs.append(second(i, pending.pop(0)))
    return outs


def _rms(x, g):
    ms = jnp.mean(x * x, axis=-1, keepdims=True)
    return x * lax.rsqrt(ms + NORM_EPS) * g


def _inproj_kernel(x_ref, g1_ref, w_ref, bd_ref, gq_a_ref, gk_a_ref, gq_b_ref, gk_b_ref,
                   cos_ref, sn_ref, sp_ref,
                   qa_ref, ka_ref, va_ref, qb_ref, kb_ref, vb_ref, *, chunk):
    def project(c):
        rows = slice(c * chunk, (c + 1) * chunk)
        u = _rms(x_ref[rows, :], g1_ref[...]).astype(BF16)
        return _dot(u, w_ref[...])

    def head_norm(t, g_row):
        w = t.shape[1]
        cw = min(w, bd_ref.shape[0])
        sq = t * t
        hi = sq.astype(BF16)
        lo = (sq - hi.astype(F32)).astype(BF16)
        bd = bd_ref[:cw, :cw]
        parts = [_dot(hi[:, j:j + cw], bd) + _dot(lo[:, j:j + cw], bd) for j in range(0, w, cw)]
        ss = parts[0] if len(parts) == 1 else jnp.concatenate(parts, axis=1)
        return t * lax.rsqrt(ss * (1.0 / HEAD_DIM) + NORM_EPS) * g_row

    def rope(t, rows):
        c, sn, sp = cos_ref[rows, :], sn_ref[rows, :], sp_ref[rows, :]
        outs = []
        for j in range(t.shape[1] // LANES):
            xc = t[:, j * LANES:(j + 1) * LANES]
            nxt = pltpu.roll(xc, LANES - 1, 1)
            prv = pltpu.roll(xc, 1, 1)
            outs.append(xc * c + nxt * sn + prv * sp)
        return outs[0] if len(outs) == 1 else jnp.concatenate(outs, axis=1)

    scale = HEAD_DIM ** -0.5 * LOG2E

    def finish(c, proj):
        rows = slice(c * chunk, (c + 1) * chunk)
        o = 0
        qa = proj[:, o:o + NA_WIDTH]; o += NA_WIDTH
        ka = proj[:, o:o + NA_WIDTH]; o += NA_WIDTH
        va = proj[:, o:o + NA_WIDTH]; o += NA_WIDTH
        qb = proj[:, o:o + GQA_WIDTH]; o += GQA_WIDTH
        kb = proj[:, o:o + GQA_KV_WIDTH]; o += GQA_KV_WIDTH
        vb = proj[:, o:o + GQA_KV_WIDTH]
        qa_ref[rows, :] = (head_norm(qa, gq_a_ref[...]) * scale).astype(BF16)
        ka_ref[rows, :] = head_norm(ka, gk_a_ref[...]).astype(BF16)
        va_ref[rows, :] = va.astype(BF16)
        qb_ref[rows, :] = (rope(head_norm(qb, gq_b_ref[...]), rows) * scale).astype(BF16)
        kb_ref[rows, :] = rope(head_norm(kb, gk_b_ref[...]), rows).astype(BF16)
        vb_ref[rows, :] = vb.astype(BF16)

    _software_pipeline(x_ref.shape[0] // chunk, project, finish)


def _inproj(x2, g1, w_bf, bd, gq_a, gk_a, gq_b, gk_b, cos_t, sn_t, sp_t, seq, tm):
    n, d = x2.shape
    in_w = w_bf.shape[1]
    per_seq = seq // tm
    row = lambda i: (i, 0)
    const = lambda i: (0, 0)
    pos = lambda i: (i % per_seq, 0)
    outs = [jax.ShapeDtypeStruct((n, w), BF16)
            for w in (NA_WIDTH, NA_WIDTH, NA_WIDTH, GQA_WIDTH, GQA_KV_WIDTH, GQA_KV_WIDTH)]
    return pl.pallas_call(
        functools.partial(_inproj_kernel, chunk=256),
        grid=(n // tm,),
        in_specs=[
            pl.BlockSpec((tm, d), row),
            pl.BlockSpec((1, d), const),
            pl.BlockSpec((d, in_w), const),
            pl.BlockSpec(bd.shape, const),
            pl.BlockSpec((1, NA_WIDTH), const),
            pl.BlockSpec((1, NA_WIDTH), const),
            pl.BlockSpec((1, GQA_WIDTH), const),
            pl.BlockSpec((1, GQA_KV_WIDTH), const),
            pl.BlockSpec((tm, LANES), pos),
            pl.BlockSpec((tm, LANES), pos),
            pl.BlockSpec((tm, LANES), pos),
        ],
        out_specs=[pl.BlockSpec((tm, s.shape[1]), row) for s in outs],
        out_shape=outs,
        compiler_params=_cparams(("parallel",)),
        name="inproj",
    )(x2, g1, w_bf, bd, gq_a, gk_a, gq_b, gk_b, cos_t, sn_t, sp_t)


def _na_kernel(q_ref, k_ref, v_ref, bias_ref, g_ref, o_ref, *, rows):
    nq = NA_QROWS * GRID_W
    nk = NA_KROWS * GRID_W
    nblk = rows // NA_QROWS
    nch = NA_WIDTH // LANES
    low = lax.broadcasted_iota(jnp.int32, (nq, LANES), 1) < HEAD_DIM
    top = lax.broadcasted_iota(jnp.int32, (LANES, nq), 0) < HEAD_DIM

    def block(jj):
        j = NA_STEP_BLOCKS * pl.program_id(1) + jj
        krow0 = jnp.clip(NA_QROWS * j - NA_WIN_H // 2, 0, rows - NA_KROWS)
        k0 = pl.multiple_of(krow0 * GRID_W, GRID_W)
        pat = jnp.clip(j, 0, 2) + jnp.clip(j - (nblk - 3), 0, 2)
        return k0, pat

    blocks = [block(jj) for jj in range(NA_STEP_BLOCKS)]

    def scores(s):
        jj, c = divmod(s, nch)
        k0, pat = blocks[jj]
        cs = slice(c * LANES, (c + 1) * LANES)
        kc = k_ref[pl.ds(k0, nk), cs]
        qc = q_ref[jj * nq:(jj + 1) * nq, cs]
        zero = jnp.zeros_like(qc)
        qbd = jnp.concatenate([jnp.where(low, qc, zero), jnp.where(low, zero, qc)], axis=0)
        return _dot_nt(kc, qbd) + bias_ref[pat, c]

    def attend(s, st):
        jj, c = divmod(s, nch)
        k0, _ = blocks[jj]
        m = jnp.max(st, axis=0, keepdims=True)
        p = jnp.exp2(st - m)
        l = jnp.sum(p, axis=0, keepdims=True)
        vc = v_ref[pl.ds(k0, nk), c * LANES:(c + 1) * LANES]
        ot = _dot_tn(vc, p.astype(BF16)) / l
        return jnp.where(top, ot[:, :nq], ot[:, nq:]).T

    outs = _software_pipeline(NA_STEP_BLOCKS * nch, scores, attend)
    for jj in range(NA_STEP_BLOCKS):
        o = jnp.concatenate(outs[jj * nch:(jj + 1) * nch], axis=1)
        o_ref[jj * nq:(jj + 1) * nq, :] = _rms(o, g_ref[...]).astype(BF16)


def _na_attn(qa, ka, va, bias, g, batch, seq):
    rows = seq // GRID_W
    nq = NA_STEP_BLOCKS * NA_QROWS * GRID_W
    nblk = seq // nq
    return pl.pallas_call(
        functools.partial(_na_kernel, rows=rows),
        grid=(batch, nblk),
        in_specs=[
            pl.BlockSpec((nq, NA_WIDTH), lambda b, j: (b * nblk + j, 0)),
            pl.BlockSpec((seq, NA_WIDTH), lambda b, j: (b, 0)),
            pl.BlockSpec((seq, NA_WIDTH), lambda b, j: (b, 0)),
            pl.BlockSpec(bias.shape, lambda b, j: (0, 0, 0, 0)),
            pl.BlockSpec((1, NA_WIDTH), lambda b, j: (0, 0)),
        ],
        out_specs=pl.BlockSpec((nq, NA_WIDTH), lambda b, j: (b * nblk + j, 0)),
        out_shape=jax.ShapeDtypeStruct(qa.shape, BF16),
        compiler_params=_cparams(("parallel", "arbitrary")),
        name="na_attn",
    )(qa, ka, va, bias, g)


def _na_bias_table(rpb, rows):
    nblk = rows // NA_QROWS
    pats = [0, 1, 2, nblk - 2, nblk - 1]
    kh, kw = NA_WIN_H, NA_WIN_W
    nh, nrel_r, nrel_c = rpb.shape
    kc = np.arange(GRID_W)[:, None]
    qc = np.arange(GRID_W)[None, :]
    cs = np.clip(qc - kw // 2, 0, GRID_W - kw)
    col_ok = (kc >= cs) & (kc < cs + kw)
    ci = np.clip(kc - qc + (kw - 1), 0, nrel_c - 1)
    pick = (ci[None] == np.arange(nrel_c)[:, None, None]) & col_ok[None]
    toep = jnp.einsum('hdj,jkq->hdkq', rpb, jnp.asarray(pick, F32), precision=lax.Precision.HIGHEST)
    toep = jnp.where(col_ok[None, None], toep * LOG2E, MASK_NEG)
    masked = jnp.full((nh, GRID_W, GRID_W), MASK_NEG, F32)
    pat_blocks = []
    for j in pats:
        krow0 = int(np.clip(NA_QROWS * j - kh // 2, 0, rows - NA_KROWS))
        key_rows = []
        for kr in range(krow0, krow0 + NA_KROWS):
            per_q = []
            for qr in range(NA_QROWS * j, NA_QROWS * (j + 1)):
                rs = int(np.clip(qr - kh // 2, 0, rows - kh))
                per_q.append(toep[:, kr - qr + kh - 1] if rs <= kr < rs + kh else masked)
            key_rows.append(jnp.stack(per_q, axis=2))
        pat_blocks.append(jnp.stack(key_rows, axis=1))
    dense = jnp.stack(pat_blocks, axis=0)
    p = len(pats)
    nk, nq = NA_KROWS * GRID_W, NA_QROWS * GRID_W
    dense = dense.reshape(p, nh // 2, 2, nk, nq).transpose(0, 1, 3, 2, 4)
    return dense.reshape(p, nh // 2, nk, 2 * nq)


def _gqa_kernel(q_ref, k_ref, v_ref, g_ref, o_ref, *, tq, bounded_scores):
    k = k_ref[...]
    v = v_ref[...]
    kv_low = lax.broadcasted_iota(jnp.int32, v.shape, 1) < HEAD_DIM
    one = jnp.ones_like(v)
    v_aug = (jnp.where(kv_low, v, one), jnp.where(kv_low, one, v))
    low = lax.broadcasted_iota(jnp.int32, (tq, LANES), 1) < HEAD_DIM
    top = lax.broadcasted_iota(jnp.int32, (LANES, tq), 0) < HEAD_DIM
    nch = GQA_WIDTH // LANES

    def scores(i):
        c, grp = divmod(i, GQA_KV_HEADS)
        qc = q_ref[:, c * LANES:(c + 1) * LANES]
        keep = low if grp == 0 else jnp.logical_not(low)
        return _dot_nt(k, jnp.where(keep, qc, jnp.zeros_like(qc)))

    def attend(i, st):
        grp = i % GQA_KV_HEADS
        if not bounded_scores:
            st = st - jnp.max(st, axis=0, keepdims=True)
        p = jnp.exp2(st).astype(BF16)
        return _dot_tn(v_aug[grp], p)

    ots = _software_pipeline(nch * GQA_KV_HEADS, scores, attend, depth=2)
    outs = []
    for c in range(nch):
        o0, o1 = ots[GQA_KV_HEADS * c], ots[GQA_KV_HEADS * c + 1]
        l0 = o0[HEAD_DIM:HEAD_DIM + 1, :]
        l1 = o1[0:1, :]
        outs.append(jnp.where(top, o0 / l0, o1 / l1).T)
    o = jnp.concatenate(outs, axis=1)
    o_ref[...] = _rms(o, g_ref[...]).astype(BF16)


def _gqa_attn(qb, kb, vb, g, batch, seq, tq, bounded_scores):
    nblk = seq // tq
    return pl.pallas_call(
        functools.partial(_gqa_kernel, tq=tq, bounded_scores=bounded_scores),
        grid=(batch, nblk),
        in_specs=[
            pl.BlockSpec((tq, GQA_WIDTH), lambda b, i: (b * nblk + i, 0)),
            pl.BlockSpec((seq, GQA_KV_WIDTH), lambda b, i: (b, 0)),
            pl.BlockSpec((seq, GQA_KV_WIDTH), lambda b, i: (b, 0)),
            pl.BlockSpec((1, GQA_WIDTH), lambda b, i: (0, 0)),
        ],
        out_specs=pl.BlockSpec((tq, GQA_WIDTH), lambda b, i: (b * nblk + i, 0)),
        out_shape=jax.ShapeDtypeStruct(qb.shape, BF16),
        compiler_params=_cparams(("parallel", "arbitrary")),
        name="gqa_attn",
    )(qb, kb, vb, g)


def _post_kernel(x_ref, ma_ref, mb_ref, wo_ref, g2_ref, wr_ref, h_ref, hn_ref, aff_ref, *, n_experts, chunk):
    def project(c):
        rows = slice(c * chunk, (c + 1) * chunk)
        return _dot(ma_ref[rows, :], wo_ref[:NA_WIDTH, :]) + _dot(mb_ref[rows, :], wo_ref[NA_WIDTH:, :])

    def route(c, attn):
        rows = slice(c * chunk, (c + 1) * chunk)
        h = x_ref[rows, :] + attn
        h_ref[rows, :] = h
        hn = _rms(h, g2_ref[...]).astype(BF16)
        hn_ref[rows, :] = hn
        logits = _dot(hn, wr_ref[...])
        lane = lax.broadcasted_iota(jnp.int32, logits.shape, 1)
        logits = jnp.where(lane < n_experts, logits, MASK_NEG)
        m = jnp.max(logits, axis=-1, keepdims=True)
        e = jnp.exp(logits - m)
        aff_ref[rows, :] = e / jnp.sum(e, axis=-1, keepdims=True)

    _software_pipeline(x_ref.shape[0] // chunk, project, route)


def _post(x2, ma, mb, wo_bf, g2, wr_bf, n_experts, tm):
    n, d = x2.shape
    row = lambda i: (i, 0)
    const = lambda i: (0, 0)
    return pl.pallas_call(
        functools.partial(_post_kernel, n_experts=n_experts, chunk=256),
        grid=(n // tm,),
        in_specs=[
            pl.BlockSpec((tm, d), row),
            pl.BlockSpec((tm, NA_WIDTH), row),
            pl.BlockSpec((tm, GQA_WIDTH), row),
            pl.BlockSpec(wo_bf.shape, const),
            pl.BlockSpec((1, d), const),
            pl.BlockSpec(wr_bf.shape, const),
        ],
        out_specs=[pl.BlockSpec((tm, d), row), pl.BlockSpec((tm, d), row), pl.BlockSpec((tm, LANES), row)],
        out_shape=[jax.ShapeDtypeStruct((n, d), F32), jax.ShapeDtypeStruct((n, d), BF16),
                   jax.ShapeDtypeStruct((n, LANES), F32)],
        compiler_params=_cparams(("parallel",)),
        name="post",
    )(x2, ma, mb, wo_bf, g2, wr_bf)


def _route_kernel(aff_ref, hn_ref, x_ref, gate_ref, posT_ref, tri_ref, aT_ref, pos_ref, *, n_experts, cap):
    seq = aff_ref.shape[0]
    rb = 256

    @pl.when(pl.program_id(0) == 0)
    def _():
        col = lax.broadcasted_iota(jnp.int32, (rb, seq), 1)
        row = lax.broadcasted_iota(jnp.int32, (rb, seq), 0)

        def fill(i, carry):
            r0 = pl.multiple_of(i * rb, rb)
            tri_ref[pl.ds(r0, rb), :] = jnp.where(row + r0 < col, 1.0, 0.0).astype(BF16)
            return carry

        lax.fori_loop(0, seq // rb, fill, 0)

    aT = aff_ref[...].T[:n_experts, :]
    def search(i, prefix):
        cand = prefix | (1 << (30 - i))
        cnt = jnp.sum(jnp.where(aT >= pltpu.bitcast(cand, F32), 1.0, 0.0), axis=1, keepdims=True)
        return jnp.where(cnt >= cap, cand, prefix)

    thr_bits = lax.fori_loop(0, 31, search, jnp.zeros((n_experts, 1), jnp.int32))
    thr = pltpu.bitcast(thr_bits, F32)
    gt = aT > thr
    eq = aT == thr
    n_gt = jnp.sum(jnp.where(gt, 1.0, 0.0), axis=1, keepdims=True)
    tri = tri_ref[...]
    eq_rank = _dot(jnp.where(eq, 1.0, 0.0).astype(BF16), tri)
    sel = gt | (eq & (eq_rank < cap - n_gt))
    pos = _dot(jnp.where(sel, 1.0, 0.0).astype(BF16), tri)
    posm = jnp.where(sel, pos, -1.0)
    pos_ref[...] = posm
    aT_ref[...] = aT
    pad = jnp.full((LANES - n_experts, seq), -1.0, F32)
    posT_ref[...] = jnp.concatenate([posm, pad], axis=0).T

    hn = hn_ref[...]
    slot = lax.broadcasted_iota(jnp.int32, (cap, seq), 0).astype(F32)

    def gather(e, carry):
        hit = pos_ref[pl.ds(e, 1), :] == slot
        onehot = jnp.where(hit, 1.0, 0.0).astype(BF16)
        x_ref[e] = _dot(onehot, hn).astype(BF16)
        gate_ref[e] = jnp.sum(jnp.where(hit, aT_ref[pl.ds(e, 1), :], 0.0), axis=1, keepdims=True)
        return carry

    lax.fori_loop(0, n_experts, gather, 0)


def _route(aff, hn, batch, seq, n_experts, cap):
    d = hn.shape[1]
    return pl.pallas_call(
        functools.partial(_route_kernel, n_experts=n_experts, cap=cap),
        grid=(batch,),
        in_specs=[
            pl.BlockSpec((seq, LANES), lambda b: (b, 0)),
            pl.BlockSpec((seq, d), lambda b: (b, 0)),
        ],
        out_specs=[
            pl.BlockSpec((n_experts, cap, d), lambda b: (0, b, 0)),
            pl.BlockSpec((n_experts, cap, 1), lambda b: (0, b, 0)),
            pl.BlockSpec((seq, LANES), lambda b: (b, 0)),
        ],
        out_shape=[
            jax.ShapeDtypeStruct((n_experts, batch * cap, d), BF16),
            jax.ShapeDtypeStruct((n_experts, batch * cap, 1), F32),
            jax.ShapeDtypeStruct((batch * seq, LANES), F32),
        ],
        scratch_shapes=[
            pltpu.VMEM((seq, seq), BF16),
            pltpu.VMEM((n_experts, seq), F32),
            pltpu.VMEM((n_experts, seq), F32),
        ],
        compiler_params=_cparams(("arbitrary",)),
        name="route",
    )(aff, hn)


def _ffn_kernel(x_ref, gate_ref, wg_ref, wu_ref, wd_ref, y_ref, acc_ref, *, chunk):
    f = pl.program_id(2)
    wg = wg_ref[...].astype(BF16)
    wu = wu_ref[...].astype(BF16)
    wd = wd_ref[...].astype(BF16)

    @pl.when(f == 0)
    def _():
        acc_ref[...] = jnp.zeros_like(acc_ref)

    def hidden(c):
        x = x_ref[c * chunk:(c + 1) * chunk, :]
        g = _dot(x, wg)
        u = _dot(x, wu)
        return (g * (1.0 / (1.0 + jnp.exp(-g))) * u).astype(BF16)

    def project(c, a):
        acc_ref[c * chunk:(c + 1) * chunk, :] += _dot(a, wd)

    _software_pipeline(x_ref.shape[0] // chunk, hidden, project)

    @pl.when(f == pl.num_programs(2) - 1)
    def _():
        y_ref[...] = (acc_ref[...] * gate_ref[...]).astype(BF16)


def _ffn(xg, gate, w_gate, w_up, w_down, tm, tf):
    e, m, d = xg.shape
    dff = w_gate.shape[2]
    return pl.pallas_call(
        functools.partial(_ffn_kernel, chunk=512),
        grid=(e, m // tm, dff // tf),
        in_specs=[
            pl.BlockSpec((None, tm, d), lambda e, i, f: (e, i, 0)),
            pl.BlockSpec((None, tm, 1), lambda e, i, f: (e, i, 0)),
            pl.BlockSpec((None, d, tf), lambda e, i, f: (e, 0, f)),
            pl.BlockSpec((None, d, tf), lambda e, i, f: (e, 0, f)),
            pl.BlockSpec((None, tf, d), lambda e, i, f: (e, f, 0)),
        ],
        out_specs=pl.BlockSpec((None, tm, d), lambda e, i, f: (e, i, 0)),
        out_shape=jax.ShapeDtypeStruct((e, m, d), BF16),
        scratch_shapes=[pltpu.VMEM((tm, d), F32)],
        compiler_params=_cparams(("parallel", "parallel", "arbitrary")),
        name="ffn",
    )(xg, gate, w_gate, w_up, w_down)


def _combine_kernel(h_ref, posT_ref, y_ref, o_ref, *, n_experts, cap):
    tt = h_ref.shape[0]
    post = posT_ref[...]
    slot = lax.broadcasted_iota(jnp.int32, (tt, cap), 1).astype(F32)
    parts = []
    for e in range(n_experts):
        hit = jnp.broadcast_to(post[:, e:e + 1], (tt, cap)) == slot
        parts.append(jnp.where(hit, 1.0, 0.0).astype(BF16))
    onehot = jnp.concatenate(parts, axis=1)
    d = y_ref.shape[2]
    y = y_ref[...].reshape(n_experts * cap, d)
    o_ref[...] = h_ref[...] + _dot(onehot, y)


def _combine(h, posT, y, batch, seq, n_experts, cap, tt):
    n, d = h.shape
    per_seq = seq // tt
    return pl.pallas_call(
        functools.partial(_combine_kernel, n_experts=n_experts, cap=cap),
        grid=(batch, per_seq),
        in_specs=[
            pl.BlockSpec((tt, d), lambda b, i: (b * per_seq + i, 0)),
            pl.BlockSpec((tt, LANES), lambda b, i: (b * per_seq + i, 0)),
            pl.BlockSpec((n_experts, cap, d), lambda b, i: (0, b, 0)),
        ],
        out_specs=pl.BlockSpec((tt, d), lambda b, i: (b * per_seq + i, 0)),
        out_shape=jax.ShapeDtypeStruct((n, d), F32),
        compiler_params=_cparams(("parallel", "arbitrary")),
        name="combine",
    )(h, posT, y)


def _rope_tables(seq):
    t = np.arange(seq)
    row = (t // GRID_W).astype(np.float32)
    col = (t % GRID_W).astype(np.float32)
    half = HEAD_DIM // 2
    inv_freq = 1.0 / (ROPE_THETA ** (jnp.arange(0, half, 2, dtype=F32) / half))
    ang = jnp.concatenate([row[:, None] * inv_freq[None], col[:, None] * inv_freq[None]], axis=-1)
    cos, sin = jnp.cos(ang), jnp.sin(ang)
    lane = np.arange(LANES)
    pair = (lane % HEAD_DIM) // 2
    even = (lane % 2 == 0)[None, :]
    cos_l = cos[:, pair]
    sin_l = sin[:, pair]
    return cos_l, jnp.where(even, -sin_l, 0.0), jnp.where(even, 0.0, sin_l)


def _layer(x2, batch, seq, norm1_g, w_in, na_q_g, na_k_g, na_rpb, gqa_q_g, gqa_k_g,
           na_out_g, gqa_out_g, w_out, norm2_g, w_router, w_gate, w_up, w_down):
    d = x2.shape[1]
    n_experts = w_router.shape[1]
    cap = EC_CAPACITY_FACTOR * seq // n_experts
    rows = seq // GRID_W

    per_kv = GQA_HEADS // GQA_KV_HEADS
    head_order = np.arange(GQA_HEADS).reshape(GQA_KV_HEADS, per_kv).T.reshape(-1)
    perm = (head_order[:, None] * HEAD_DIM + np.arange(HEAD_DIM)[None, :]).reshape(-1)
    qb0 = 3 * NA_WIDTH
    w_in_p = jnp.concatenate([w_in[:, :qb0], w_in[:, qb0 + perm], w_in[:, qb0 + GQA_WIDTH:]], axis=1)
    w_out_p = jnp.concatenate([w_out[:NA_WIDTH], w_out[NA_WIDTH + perm]], axis=0)
    gqa_out_g_p = gqa_out_g[perm]

    hd = np.arange(MXU_TILE) // HEAD_DIM
    bd = jnp.asarray(hd[:, None] == hd[None, :], dtype=BF16)
    cos_t, sn_t, sp_t = _rope_tables(seq)
    tile_g = lambda g, heads: jnp.tile(g, heads)[None, :]

    qa, ka, va, qb, kb, vb = _inproj(
        x2, norm1_g[None, :], w_in_p.astype(BF16), bd,
        tile_g(na_q_g, NA_HEADS), tile_g(na_k_g, NA_HEADS),
        tile_g(gqa_q_g, GQA_HEADS), tile_g(gqa_k_g, GQA_KV_HEADS),
        cos_t, sn_t, sp_t, seq, tm=1024)

    bias = _na_bias_table(na_rpb, rows)
    ma = _na_attn(qa, ka, va, bias, na_out_g[None, :], batch, seq)
    bound = 1.02 * HEAD_DIM ** 0.5 * LOG2E * jnp.max(jnp.abs(gqa_q_g)) * jnp.max(jnp.abs(gqa_k_g))
    gqa = lambda bounded: functools.partial(_gqa_attn, batch=batch, seq=seq, tq=512, bounded_scores=bounded)
    mb = lax.cond(bound <= GQA_SCORE_LIMIT, gqa(True), gqa(False), qb, kb, vb, gqa_out_g_p[None, :])

    wr = jnp.zeros((d, LANES), F32).at[:, :n_experts].set(w_router).astype(BF16)
    h, hn, aff = _post(x2, ma, mb, w_out_p.astype(BF16), norm2_g[None, :], wr, n_experts, tm=1024)

    xg, gate, posT = _route(aff, hn, batch, seq, n_experts, cap)
    y = _ffn(xg, gate, w_gate, w_up, w_down, tm=2048, tf=256)
    return _combine(h, posT, y, batch, seq, n_experts, cap, tt=512)


def kernel(x, norm1_g, w_in, na_q_norm_g, na_k_norm_g, na_rpb, gqa_q_norm_g, gqa_k_norm_g,
           na_out_g, gqa_out_g, w_out, norm2_g, w_router, w_gate, w_up, w_down):
    batch, seq, d = x.shape
    x2 = x.reshape(batch * seq, d)
    for l in range(norm1_g.shape[0]):
        x2 = _layer(x2, batch, seq, norm1_g[l], w_in[l], na_q_norm_g[l], na_k_norm_g[l], na_rpb[l],
                    gqa_q_norm_g[l], gqa_k_norm_g[l], na_out_g[l], gqa_out_g[l], w_out[l],
                    norm2_g[l], w_router[l], w_gate[l], w_up[l], w_down[l])
    return x2.reshape(batch, seq, d)
```

```python
import functools

import jax
import jax.numpy as jnp
import numpy as np
from jax import lax
from jax.experimental import pallas as pl
from jax.experimental.pallas import tpu as pltpu

F32 = jnp.float32
BF16 = jnp.bfloat16

GRID_W = 64
HEAD_DIM = 64
NA_HEADS = 8
NA_WIN_H = 8
NA_WIN_W = 16
GQA_HEADS = 8
GQA_KV_HEADS = 2
ROPE_THETA = 10000.0
N_EXPERTS = 16
EC_CAPACITY_FACTOR = 2
NORM_EPS = 1e-6

LANES = 128
MXU_TILE = 256
NA_WIDTH = NA_HEADS * HEAD_DIM
GQA_WIDTH = GQA_HEADS * HEAD_DIM
GQA_KV_WIDTH = GQA_KV_HEADS * HEAD_DIM
NA_QROWS = 2
NA_KROWS = NA_WIN_H + NA_QROWS
NA_STEP_BLOCKS = 4
MASK_NEG = -1e30
LOG2E = 1.4426950408889634
SCORE_LIMIT = 32.0

VMEM_LIMIT = 56 * 1024 * 1024


def _cparams(sem):
    return pltpu.CompilerParams(dimension_semantics=sem, vmem_limit_bytes=VMEM_LIMIT)


def _dot(a, b):
    return jnp.dot(a, b, preferred_element_type=F32)


def _dot_nt(a, b):
    return lax.dot_general(a, b, (((1,), (1,)), ((), ())), preferred_element_type=F32)


def _dot_tn(a, b):
    return lax.dot_general(a, b, (((0,), (0,)), ((), ())), preferred_element_type=F32)


def _software_pipeline(n, first, second, depth=1):
    pending = [first(i) for i in range(min(depth, n))]
    outs = []
    for i in range(n):
        if i + depth < n:
            pending.append(first(i + depth))
        outs.append(second(i, pending.pop(0)))
    return outs


def _rms(x, g):
    ms = jnp.mean(x * x, axis=-1, keepdims=True)
    return x * lax.rsqrt(ms + NORM_EPS) * g


def _inproj_kernel(x_ref, g1_ref, w_ref, bd_ref, gq_a_ref, gk_a_ref, gq_b_ref, gk_b_ref,
                   cos_ref, sn_ref, sp_ref,
                   qa_ref, ka_ref, va_ref, qb_ref, kb_ref, vb_ref, *, chunk):
    def project(c):
        rows = slice(c * chunk, (c + 1) * chunk)
        u = _rms(x_ref[rows, :], g1_ref[...]).astype(BF16)
        return _dot(u, w_ref[...])

    def head_norm(t, g_row):
        w = t.shape[1]
        cw = min(w, bd_ref.shape[0])
        sq = t * t
        hi = sq.astype(BF16)
        lo = (sq - hi.astype(F32)).astype(BF16)
        bd = bd_ref[:cw, :cw]
        parts = [_dot(hi[:, j:j + cw], bd) + _dot(lo[:, j:j + cw], bd) for j in range(0, w, cw)]
        ss = parts[0] if len(parts) == 1 else jnp.concatenate(parts, axis=1)
        return t * lax.rsqrt(ss * (1.0 / HEAD_DIM) + NORM_EPS) * g_row

    def rope(t, rows):
        c, sn, sp = cos_ref[rows, :], sn_ref[rows, :], sp_ref[rows, :]
        outs = []
        for j in range(t.shape[1] // LANES):
            xc = t[:, j * LANES:(j + 1) * LANES]
            nxt = pltpu.roll(xc, LANES - 1, 1)
            prv = pltpu.roll(xc, 1, 1)
            outs.append(xc * c + nxt * sn + prv * sp)
        return outs[0] if len(outs) == 1 else jnp.concatenate(outs, axis=1)

    scale = HEAD_DIM ** -0.5 * LOG2E

    def finish(c, proj):
        rows = slice(c * chunk, (c + 1) * chunk)
        o = 0
        qa = proj[:, o:o + NA_WIDTH]; o += NA_WIDTH
        ka = proj[:, o:o + NA_WIDTH]; o += NA_WIDTH
        va = proj[:, o:o + NA_WIDTH]; o += NA_WIDTH
        qb = proj[:, o:o + GQA_WIDTH]; o += GQA_WIDTH
        kb = proj[:, o:o + GQA_KV_WIDTH]; o += GQA_KV_WIDTH
        vb = proj[:, o:o + GQA_KV_WIDTH]
        qa_ref[rows, :] = (head_norm(qa, gq_a_ref[...]) * scale).astype(BF16)
        ka_ref[rows, :] = head_norm(ka, gk_a_ref[...]).astype(BF16)
        va_ref[rows, :] = va.astype(BF16)
        qb_ref[rows, :] = (rope(head_norm(qb, gq_b_ref[...]), rows) * scale).astype(BF16)
        kb_ref[rows, :] = rope(head_norm(kb, gk_b_ref[...]), rows).astype(BF16)
        vb_ref[rows, :] = vb.astype(BF16)

    _software_pipeline(x_ref.shape[0] // chunk, project, finish)


def _inproj(x2, g1, w_bf, bd, gq_a, gk_a, gq_b, gk_b, cos_t, sn_t, sp_t, seq, tm):
    n, d = x2.shape
    in_w = w_bf.shape[1]
    per_seq = seq // tm
    row = lambda i: (i, 0)
    const = lambda i: (0, 0)
    pos = lambda i: (i % per_seq, 0)
    outs = [jax.ShapeDtypeStruct((n, w), BF16)
            for w in (NA_WIDTH, NA_WIDTH, NA_WIDTH, GQA_WIDTH, GQA_KV_WIDTH, GQA_KV_WIDTH)]
    return pl.pallas_call(
        functools.partial(_inproj_kernel, chunk=256),
        grid=(n // tm,),
        in_specs=[
            pl.BlockSpec((tm, d), row),
            pl.BlockSpec((1, d), const),
            pl.BlockSpec((d, in_w), const),
            pl.BlockSpec(bd.shape, const),
            pl.BlockSpec((1, NA_WIDTH), const),
            pl.BlockSpec((1, NA_WIDTH), const),
            pl.BlockSpec((1, GQA_WIDTH), const),
            pl.BlockSpec((1, GQA_KV_WIDTH), const),
            pl.BlockSpec((tm, LANES), pos),
            pl.BlockSpec((tm, LANES), pos),
            pl.BlockSpec((tm, LANES), pos),
        ],
        out_specs=[pl.BlockSpec((tm, s.shape[1]), row) for s in outs],
        out_shape=outs,
        compiler_params=_cparams(("parallel",)),
        name="inproj",
    )(x2, g1, w_bf, bd, gq_a, gk_a, gq_b, gk_b, cos_t, sn_t, sp_t)


def _na_kernel(q_ref, k_ref, v_ref, bias_ref, g_ref, o_ref, *, rows, bounded_scores):
    nq = NA_QROWS * GRID_W
    nk = NA_KROWS * GRID_W
    nblk = rows // NA_QROWS
    nch = NA_WIDTH // LANES
    low = lax.broadcasted_iota(jnp.int32, (nq, LANES), 1) < HEAD_DIM
    top = lax.broadcasted_iota(jnp.int32, (LANES, nq), 0) < HEAD_DIM

    def block(jj):
        j = NA_STEP_BLOCKS * pl.program_id(1) + jj
        krow0 = jnp.clip(NA_QROWS * j - NA_WIN_H // 2, 0, rows - NA_KROWS)
        k0 = pl.multiple_of(krow0 * GRID_W, GRID_W)
        pat = jnp.clip(j, 0, 2) + jnp.clip(j - (nblk - 3), 0, 2)
        return k0, pat

    blocks = [block(jj) for jj in range(NA_STEP_BLOCKS)]

    def scores(s):
        jj, c = divmod(s, nch)
        k0, pat = blocks[jj]
        cs = slice(c * LANES, (c + 1) * LANES)
        kc = k_ref[pl.ds(k0, nk), cs]
        qc = q_ref[jj * nq:(jj + 1) * nq, cs]
        zero = jnp.zeros_like(qc)
        qbd = jnp.concatenate([jnp.where(low, qc, zero), jnp.where(low, zero, qc)], axis=0)
        return _dot_nt(kc, qbd) + bias_ref[pat, c]

    def attend(s, st):
        jj, c = divmod(s, nch)
        k0, _ = blocks[jj]
        if not bounded_scores:
            st = st - jnp.max(st, axis=0, keepdims=True)
        p = jnp.exp2(st)
        l = jnp.sum(p, axis=0, keepdims=True)
        vc = v_ref[pl.ds(k0, nk), c * LANES:(c + 1) * LANES]
        ot = _dot_tn(vc, p.astype(BF16)) / l
        return jnp.where(top, ot[:, :nq], ot[:, nq:]).T

    outs = _software_pipeline(NA_STEP_BLOCKS * nch, scores, attend)
    for jj in range(NA_STEP_BLOCKS):
        o = jnp.concatenate(outs[jj * nch:(jj + 1) * nch], axis=1)
        o_ref[jj * nq:(jj + 1) * nq, :] = _rms(o, g_ref[...]).astype(BF16)


def _na_attn(qa, ka, va, bias, g, batch, seq, bounded_scores):
    rows = seq // GRID_W
    nq = NA_STEP_BLOCKS * NA_QROWS * GRID_W
    nblk = seq // nq
    return pl.pallas_call(
        functools.partial(_na_kernel, rows=rows, bounded_scores=bounded_scores),
        grid=(batch, nblk),
        in_specs=[
            pl.BlockSpec((nq, NA_WIDTH), lambda b, j: (b * nblk + j, 0)),
            pl.BlockSpec((seq, NA_WIDTH), lambda b, j: (b, 0)),
            pl.BlockSpec((seq, NA_WIDTH), lambda b, j: (b, 0)),
            pl.BlockSpec(bias.shape, lambda b, j: (0, 0, 0, 0)),
            pl.BlockSpec((1, NA_WIDTH), lambda b, j: (0, 0)),
        ],
        out_specs=pl.BlockSpec((nq, NA_WIDTH), lambda b, j: (b * nblk + j, 0)),
        out_shape=jax.ShapeDtypeStruct(qa.shape, BF16),
        compiler_params=_cparams(("parallel", "arbitrary")),
        name="na_attn",
    )(qa, ka, va, bias, g)


def _na_bias_table(rpb, rows):
    nblk = rows // NA_QROWS
    pats = [0, 1, 2, nblk - 2, nblk - 1]
    kh, kw = NA_WIN_H, NA_WIN_W
    nh, nrel_r, nrel_c = rpb.shape
    kc = np.arange(GRID_W)[:, None]
    qc = np.arange(GRID_W)[None, :]
    cs = np.clip(qc - kw // 2, 0, GRID_W - kw)
    col_ok = (kc >= cs) & (kc < cs + kw)
    ci = np.clip(kc - qc + (kw - 1), 0, nrel_c - 1)
    pick = (ci[None] == np.arange(nrel_c)[:, None, None]) & col_ok[None]
    toep = jnp.einsum('hdj,jkq->hdkq', rpb, jnp.asarray(pick, F32), precision=lax.Precision.HIGHEST)
    toep = jnp.where(col_ok[None, None], toep * LOG2E, MASK_NEG)
    masked = jnp.full((nh, GRID_W, GRID_W), MASK_NEG, F32)
    pat_blocks = []
    for j in pats:
        krow0 = int(np.clip(NA_QROWS * j - kh // 2, 0, rows - NA_KROWS))
        key_rows = []
        for kr in range(krow0, krow0 + NA_KROWS):
            per_q = []
            for qr in range(NA_QROWS * j, NA_QROWS * (j + 1)):
                rs = int(np.clip(qr - kh // 2, 0, rows - kh))
                per_q.append(toep[:, kr - qr + kh - 1] if rs <= kr < rs + kh else masked)
            key_rows.append(jnp.stack(per_q, axis=2))
        pat_blocks.append(jnp.stack(key_rows, axis=1))
    dense = jnp.stack(pat_blocks, axis=0)
    p = len(pats)
    nk, nq = NA_KROWS * GRID_W, NA_QROWS * GRID_W
    dense = dense.reshape(p, nh // 2, 2, nk, nq).transpose(0, 1, 3, 2, 4)
    return dense.reshape(p, nh // 2, nk, 2 * nq)


def _gqa_kernel(q_ref, k_ref, v_ref, g_ref, o_ref, *, tq, bounded_scores):
    k = k_ref[...]
    v = v_ref[...]
    kv_low = lax.broadcasted_iota(jnp.int32, v.shape, 1) < HEAD_DIM
    one = jnp.ones_like(v)
    v_aug = (jnp.where(kv_low, v, one), jnp.where(kv_low, one, v))
    low = lax.broadcasted_iota(jnp.int32, (tq, LANES), 1) < HEAD_DIM
    top = lax.broadcasted_iota(jnp.int32, (LANES, tq), 0) < HEAD_DIM
    nch = GQA_WIDTH // LANES

    def scores(i):
        c, grp = divmod(i, GQA_KV_HEADS)
        qc = q_ref[:, c * LANES:(c + 1) * LANES]
        keep = low if grp == 0 else jnp.logical_not(low)
        return _dot_nt(k, jnp.where(keep, qc, jnp.zeros_like(qc)))

    def attend(i, st):
        grp = i % GQA_KV_HEADS
        if not bounded_scores:
            st = st - jnp.max(st, axis=0, keepdims=True)
        p = jnp.exp2(st).astype(BF16)
        return _dot_tn(v_aug[grp], p)

    ots = _software_pipeline(nch * GQA_KV_HEADS, scores, attend, depth=2)
    outs = []
    for c in range(nch):
        o0, o1 = ots[GQA_KV_HEADS * c], ots[GQA_KV_HEADS * c + 1]
        l0 = o0[HEAD_DIM:HEAD_DIM + 1, :]
        l1 = o1[0:1, :]
        outs.append(jnp.where(top, o0 / l0, o1 / l1).T)
    o = jnp.concatenate(outs, axis=1)
    o_ref[...] = _rms(o, g_ref[...]).astype(BF16)


def _gqa_attn(qb, kb, vb, g, batch, seq, tq, bounded_scores):
    nblk = seq // tq
    return pl.pallas_call(
        functools.partial(_gqa_kernel, tq=tq, bounded_scores=bounded_scores),
        grid=(batch, nblk),
        in_specs=[
            pl.BlockSpec((tq, GQA_WIDTH), lambda b, i: (b * nblk + i, 0)),
            pl.BlockSpec((seq, GQA_KV_WIDTH), lambda b, i: (b, 0)),
            pl.BlockSpec((seq, GQA_KV_WIDTH), lambda b, i: (b, 0)),
            pl.BlockSpec((1, GQA_WIDTH), lambda b, i: (0, 0)),
        ],
        out_specs=pl.BlockSpec((tq, GQA_WIDTH), lambda b, i: (b * nblk + i, 0)),
        out_shape=jax.ShapeDtypeStruct(qb.shape, BF16),
        compiler_params=_cparams(("parallel", "arbitrary")),
        name="gqa_attn",
    )(qb, kb, vb, g)


def _post_kernel(x_ref, ma_ref, mb_ref, wo_ref, g2_ref, wr_ref, h_ref, hn_ref, aff_ref, *, n_experts, chunk):
    def project(c):
        rows = slice(c * chunk, (c + 1) * chunk)
        return _dot(ma_ref[rows, :], wo_ref[:NA_WIDTH, :]) + _dot(mb_ref[rows, :], wo_ref[NA_WIDTH:, :])

    def route(c, attn):
        rows = slice(c * chunk, (c + 1) * chunk)
        h = x_ref[rows, :] + attn
        h_ref[rows, :] = h
        hn = _rms(h, g2_ref[...]).astype(BF16)
        hn_ref[rows, :] = hn
        logits = _dot(hn, wr_ref[...])
        lane = lax.broadcasted_iota(jnp.int32, logits.shape, 1)
        logits = jnp.where(lane < n_experts, logits, MASK_NEG)
        m = jnp.max(logits, axis=-1, keepdims=True)
        e = jnp.exp(logits - m)
        aff_ref[rows, :] = e / jnp.sum(e, axis=-1, keepdims=True)

    _software_pipeline(x_ref.shape[0] // chunk, project, route)


def _post(x2, ma, mb, wo_bf, g2, wr_bf, n_experts, tm):
    n, d = x2.shape
    row = lambda i: (i, 0)
    const = lambda i: (0, 0)
    return pl.pallas_call(
        functools.partial(_post_kernel, n_experts=n_experts, chunk=256),
        grid=(n // tm,),
        in_specs=[
            pl.BlockSpec((tm, d), row),
            pl.BlockSpec((tm, NA_WIDTH), row),
            pl.BlockSpec((tm, GQA_WIDTH), row),
            pl.BlockSpec(wo_bf.shape, const),
            pl.BlockSpec((1, d), const),
            pl.BlockSpec(wr_bf.shape, const),
        ],
        out_specs=[pl.BlockSpec((tm, d), row), pl.BlockSpec((tm, d), row), pl.BlockSpec((tm, LANES), row)],
        out_shape=[jax.ShapeDtypeStruct((n, d), F32), jax.ShapeDtypeStruct((n, d), BF16),
                   jax.ShapeDtypeStruct((n, LANES), F32)],
        compiler_params=_cparams(("parallel",)),
        name="post",
    )(x2, ma, mb, wo_bf, g2, wr_bf)


def _route_kernel(aff_ref, hn_ref, x_ref, gate_ref, posT_ref, tri_ref, aT_ref, pos_ref, *, n_experts, cap):
    seq = aff_ref.shape[0]
    rb = 256

    @pl.when(pl.program_id(0) == 0)
    def _():
        col = lax.broadcasted_iota(jnp.int32, (rb, seq), 1)
        row = lax.broadcasted_iota(jnp.int32, (rb, seq), 0)

        def fill(i, carry):
            r0 = pl.multiple_of(i * rb, rb)
            tri_ref[pl.ds(r0, rb), :] = jnp.where(row + r0 < col, 1.0, 0.0).astype(BF16)
            return carry

        lax.fori_loop(0, seq // rb, fill, 0)

    aT = aff_ref[...].T[:n_experts, :]
    def search(i, prefix):
        cand = prefix | (1 << (30 - i))
        cnt = jnp.sum(jnp.where(aT >= pltpu.bitcast(cand, F32), 1.0, 0.0), axis=1, keepdims=True)
        return jnp.where(cnt >= cap, cand, prefix)

    thr_bits = lax.fori_loop(0, 31, search, jnp.zeros((n_experts, 1), jnp.int32))
    thr = pltpu.bitcast(thr_bits, F32)
    gt = aT > thr
    eq = aT == thr
    n_gt = jnp.sum(jnp.where(gt, 1.0, 0.0), axis=1, keepdims=True)
    tri = tri_ref[...]
    eq_rank = _dot(jnp.where(eq, 1.0, 0.0).astype(BF16), tri)
    sel = gt | (eq & (eq_rank < cap - n_gt))
    pos = _dot(jnp.where(sel, 1.0, 0.0).astype(BF16), tri)
    posm = jnp.where(sel, pos, -1.0)
    pos_ref[...] = posm
    aT_ref[...] = aT
    pad = jnp.full((LANES - n_experts, seq), -1.0, F32)
    posT_ref[...] = jnp.concatenate([posm, pad], axis=0).T

    hn = hn_ref[...]
    slot = lax.broadcasted_iota(jnp.int32, (cap, seq), 0).astype(F32)

    def gather(e, carry):
        hit = pos_ref[pl.ds(e, 1), :] == slot
        onehot = jnp.where(hit, 1.0, 0.0).astype(BF16)
        x_ref[e] = _dot(onehot, hn).astype(BF16)
        gate_ref[e] = jnp.sum(jnp.where(hit, aT_ref[pl.ds(e, 1), :], 0.0), axis=1, keepdims=True)
        return carry

    lax.fori_loop(0, n_experts, gather, 0)


def _route(aff, hn, batch, seq, n_experts, cap):
    d = hn.shape[1]
    return pl.pallas_call(
        functools.partial(_route_kernel, n_experts=n_experts, cap=cap),
        grid=(batch,),
        in_specs=[
            pl.BlockSpec((seq, LANES), lambda b: (b, 0)),
            pl.BlockSpec((seq, d), lambda b: (b, 0)),
        ],
        out_specs=[
            pl.BlockSpec((n_experts, cap, d), lambda b: (0, b, 0)),
            pl.BlockSpec((n_experts, cap, 1), lambda b: (0, b, 0)),
            pl.BlockSpec((seq, LANES), lambda b: (b, 0)),
        ],
        out_shape=[
            jax.ShapeDtypeStruct((n_experts, batch * cap, d), BF16),
            jax.ShapeDtypeStruct((n_experts, batch * cap, 1), F32),
            jax.ShapeDtypeStruct((batch * seq, LANES), F32),
        ],
        scratch_shapes=[
            pltpu.VMEM((seq, seq), BF16),
            pltpu.VMEM((n_experts, seq), F32),
            pltpu.VMEM((n_experts, seq), F32),
        ],
        compiler_params=_cparams(("arbitrary",)),
        name="route",
    )(aff, hn)


def _ffn_kernel(x_ref, gate_ref, wg_ref, wu_ref, wd_ref, y_ref, acc_ref, *, chunk):
    f = pl.program_id(2)
    wg = wg_ref[...].astype(BF16)
    wu = wu_ref[...].astype(BF16)
    wd = wd_ref[...].astype(BF16)

    @pl.when(f == 0)
    def _():
        acc_ref[...] = jnp.zeros_like(acc_ref)

    def hidden(c):
        x = x_ref[c * chunk:(c + 1) * chunk, :]
        g = _dot(x, wg)
        u = _dot(x, wu)
        return (g * (1.0 / (1.0 + jnp.exp(-g))) * u).astype(BF16)

    def project(c, a):
        acc_ref[c * chunk:(c + 1) * chunk, :] += _dot(a, wd)

    _software_pipeline(x_ref.shape[0] // chunk, hidden, project)

    @pl.when(f == pl.num_programs(2) - 1)
    def _():
        y_ref[...] = (acc_ref[...] * gate_ref[...]).astype(BF16)


def _ffn(xg, gate, w_gate, w_up, w_down, tm, tf):
    e, m, d = xg.shape
    dff = w_gate.shape[2]
    return pl.pallas_call(
        functools.partial(_ffn_kernel, chunk=512),
        grid=(e, m // tm, dff // tf),
        in_specs=[
            pl.BlockSpec((None, tm, d), lambda e, i, f: (e, i, 0)),
            pl.BlockSpec((None, tm, 1), lambda e, i, f: (e, i, 0)),
            pl.BlockSpec((None, d, tf), lambda e, i, f: (e, 0, f)),
            pl.BlockSpec((None, d, tf), lambda e, i, f: (e, 0, f)),
            pl.BlockSpec((None, tf, d), lambda e, i, f: (e, f, 0)),
        ],
        out_specs=pl.BlockSpec((None, tm, d), lambda e, i, f: (e, i, 0)),
        out_shape=jax.ShapeDtypeStruct((e, m, d), BF16),
        scratch_shapes=[pltpu.VMEM((tm, d), F32)],
        compiler_params=_cparams(("parallel", "parallel", "arbitrary")),
        name="ffn",
    )(xg, gate, w_gate, w_up, w_down)


def _combine_kernel(h_ref, posT_ref, y_ref, o_ref, *, n_experts, cap):
    tt = h_ref.shape[0]
    post = posT_ref[...]
    slot = lax.broadcasted_iota(jnp.int32, (tt, cap), 1).astype(F32)
    parts = []
    for e in range(n_experts):
        hit = jnp.broadcast_to(post[:, e:e + 1], (tt, cap)) == slot
        parts.append(jnp.where(hit, 1.0, 0.0).astype(BF16))
    onehot = jnp.concatenate(parts, axis=1)
    d = y_ref.shape[2]
    y = y_ref[...].reshape(n_experts * cap, d)
    o_ref[...] = h_ref[...] + _dot(onehot, y)


def _combine(h, posT, y, batch, seq, n_experts, cap, tt):
    n, d = h.shape
    per_seq = seq // tt
    return pl.pallas_call(
        functools.partial(_combine_kernel, n_experts=n_experts, cap=cap),
        grid=(batch, per_seq),
        in_specs=[
            pl.BlockSpec((tt, d), lambda b, i: (b * per_seq + i, 0)),
            pl.BlockSpec((tt, LANES), lambda b, i: (b * per_seq + i, 0)),
            pl.BlockSpec((n_experts, cap, d), lambda b, i: (0, b, 0)),
        ],
        out_specs=pl.BlockSpec((tt, d), lambda b, i: (b * per_seq + i, 0)),
        out_shape=jax.ShapeDtypeStruct((n, d), F32),
        compiler_params=_cparams(("parallel", "arbitrary")),
        name="combine",
    )(h, posT, y)


def _rope_tables(seq):
    t = np.arange(seq)
    row = (t // GRID_W).astype(np.float32)
    col = (t % GRID_W).astype(np.float32)
    half = HEAD_DIM // 2
    inv_freq = 1.0 / (ROPE_THETA ** (jnp.arange(0, half, 2, dtype=F32) / half))
    ang = jnp.concatenate([row[:, None] * inv_freq[None], col[:, None] * inv_freq[None]], axis=-1)
    cos, sin = jnp.cos(ang), jnp.sin(ang)
    lane = np.arange(LANES)
    pair = (lane % HEAD_DIM) // 2
    even = (lane % 2 == 0)[None, :]
    cos_l = cos[:, pair]
    sin_l = sin[:, pair]
    return cos_l, jnp.where(even, -sin_l, 0.0), jnp.where(even, 0.0, sin_l)


def _layer(x2, batch, seq, norm1_g, w_in, na_q_g, na_k_g, na_rpb, gqa_q_g, gqa_k_g,
           na_out_g, gqa_out_g, w_out, norm2_g, w_router, w_gate, w_up, w_down):
    d = x2.shape[1]
    n_experts = w_router.shape[1]
    cap = EC_CAPACITY_FACTOR * seq // n_experts
    rows = seq // GRID_W

    per_kv = GQA_HEADS // GQA_KV_HEADS
    head_order = np.arange(GQA_HEADS).reshape(GQA_KV_HEADS, per_kv).T.reshape(-1)
    perm = (head_order[:, None] * HEAD_DIM + np.arange(HEAD_DIM)[None, :]).reshape(-1)
    qb0 = 3 * NA_WIDTH
    w_in_p = jnp.concatenate([w_in[:, :qb0], w_in[:, qb0 + perm], w_in[:, qb0 + GQA_WIDTH:]], axis=1)
    w_out_p = jnp.concatenate([w_out[:NA_WIDTH], w_out[NA_WIDTH + perm]], axis=0)
    gqa_out_g_p = gqa_out_g[perm]

    hd = np.arange(MXU_TILE) // HEAD_DIM
    bd = jnp.asarray(hd[:, None] == hd[None, :], dtype=BF16)
    cos_t, sn_t, sp_t = _rope_tables(seq)
    tile_g = lambda g, heads: jnp.tile(g, heads)[None, :]

    qa, ka, va, qb, kb, vb = _inproj(
        x2, norm1_g[None, :], w_in_p.astype(BF16), bd,
        tile_g(na_q_g, NA_HEADS), tile_g(na_k_g, NA_HEADS),
        tile_g(gqa_q_g, GQA_HEADS), tile_g(gqa_k_g, GQA_KV_HEADS),
        cos_t, sn_t, sp_t, seq, tm=1024)

    bias = _na_bias_table(na_rpb, rows)
    qk_bound = lambda gq, gk: 1.02 * HEAD_DIM ** 0.5 * LOG2E * jnp.max(jnp.abs(gq)) * jnp.max(jnp.abs(gk))
    na_bound = qk_bound(na_q_g, na_k_g) + LOG2E * jnp.max(jnp.abs(na_rpb))
    na = lambda bounded: functools.partial(_na_attn, batch=batch, seq=seq, bounded_scores=bounded)
    ma = lax.cond(na_bound <= SCORE_LIMIT, na(True), na(False), qa, ka, va, bias, na_out_g[None, :])
    bound = qk_bound(gqa_q_g, gqa_k_g)
    gqa = lambda bounded: functools.partial(_gqa_attn, batch=batch, seq=seq, tq=512, bounded_scores=bounded)
    mb = lax.cond(bound <= SCORE_LIMIT, gqa(True), gqa(False), qb, kb, vb, gqa_out_g_p[None, :])

    wr = jnp.zeros((d, LANES), F32).at[:, :n_experts].set(w_router).astype(BF16)
    h, hn, aff = _post(x2, ma, mb, w_out_p.astype(BF16), norm2_g[None, :], wr, n_experts, tm=1024)

    xg, gate, posT = _route(aff, hn, batch, seq, n_experts, cap)
    y = _ffn(xg, gate, w_gate, w_up, w_down, tm=2048, tf=256)
    return _combine(h, posT, y, batch, seq, n_experts, cap, tt=512)


def kernel(x, norm1_g, w_in, na_q_norm_g, na_k_norm_g, na_rpb, gqa_q_norm_g, gqa_k_norm_g,
           na_out_g, gqa_out_g, w_out, norm2_g, w_router, w_gate, w_up, w_down):
    batch, seq, d = x.shape
    x2 = x.reshape(batch * seq, d)
    for l in range(norm1_g.shape[0]):
        x2 = _layer(x2, batch, seq, norm1_g[l], w_in[l], na_q_norm_g[l], na_k_norm_g[l], na_rpb[l],
                    gqa_q_norm_g[l], gqa_k_norm_g[l], na_out_g[l], gqa_out_g[l], w_out[l],
                    norm2_g[l], w_router[l], w_gate[l], w_up[l], w_down[l])
    return x2.reshape(batch, seq, d)
```

```python
import functools

import jax
import jax.numpy as jnp
import numpy as np
from jax import lax
from jax.experimental import pallas as pl
from jax.experimental.pallas import tpu as pltpu

F32 = jnp.float32
BF16 = jnp.bfloat16

GRID_W = 64
HEAD_DIM = 64
NA_HEADS = 8
NA_WIN_H = 8
NA_WIN_W = 16
GQA_HEADS = 8
GQA_KV_HEADS = 2
ROPE_THETA = 10000.0
N_EXPERTS = 16
EC_CAPACITY_FACTOR = 2
NORM_EPS = 1e-6

LANES = 128
MXU_TILE = 256
NA_WIDTH = NA_HEADS * HEAD_DIM
GQA_WIDTH = GQA_HEADS * HEAD_DIM
GQA_KV_WIDTH = GQA_KV_HEADS * HEAD_DIM
NA_QROWS = 2
NA_KROWS = NA_WIN_H + NA_QROWS
NA_STEP_BLOCKS = 4
MASK_NEG = -1e30
LOG2E = 1.4426950408889634
SCORE_LIMIT = 32.0

VMEM_LIMIT = 56 * 1024 * 1024


def _cparams(sem):
    return pltpu.CompilerParams(dimension_semantics=sem, vmem_limit_bytes=VMEM_LIMIT)


def _dot(a, b):
    return jnp.dot(a, b, preferred_element_type=F32)


def _dot_nt(a, b):
    return lax.dot_general(a, b, (((1,), (1,)), ((), ())), preferred_element_type=F32)


def _dot_tn(a, b):
    return lax.dot_general(a, b, (((0,), (0,)), ((), ())), preferred_element_type=F32)


def _software_pipeline(n, first, second, depth=1):
    pending = [first(i) for i in range(min(depth, n))]
    outs = []
    for i in range(n):
        if i + depth < n:
            pending.append(first(i + depth))
        outs.append(second(i, pending.pop(0)))
    return outs


def _rms(x, g):
    ms = jnp.mean(x * x, axis=-1, keepdims=True)
    return x * lax.rsqrt(ms + NORM_EPS) * g


def _inproj_kernel(x_ref, g1_ref, w_ref, bd_ref, gq_a_ref, gk_a_ref, gq_b_ref, gk_b_ref,
                   cos_ref, sn_ref, sp_ref,
                   qa_ref, ka_ref, va_ref, qb_ref, kb_ref, vb_ref, *, chunk):
    def project(c):
        rows = slice(c * chunk, (c + 1) * chunk)
        u = _rms(x_ref[rows, :], g1_ref[...]).astype(BF16)
        return _dot(u, w_ref[...])

    def head_norm(t, g_row):
        w = t.shape[1]
        cw = min(w, bd_ref.shape[0])
        sq = t * t
        hi = sq.astype(BF16)
        lo = (sq - hi.astype(F32)).astype(BF16)
        bd = bd_ref[:cw, :cw]
        parts = [_dot(hi[:, j:j + cw], bd) + _dot(lo[:, j:j + cw], bd) for j in range(0, w, cw)]
        ss = parts[0] if len(parts) == 1 else jnp.concatenate(parts, axis=1)
        return t * lax.rsqrt(ss * (1.0 / HEAD_DIM) + NORM_EPS) * g_row

    def rope(t, rows):
        c, sn, sp = cos_ref[rows, :], sn_ref[rows, :], sp_ref[rows, :]
        outs = []
        for j in range(t.shape[1] // LANES):
            xc = t[:, j * LANES:(j + 1) * LANES]
            nxt = pltpu.roll(xc, LANES - 1, 1)
            prv = pltpu.roll(xc, 1, 1)
            outs.append(xc * c + nxt * sn + prv * sp)
        return outs[0] if len(outs) == 1 else jnp.concatenate(outs, axis=1)

    scale = HEAD_DIM ** -0.5 * LOG2E

    def finish(c, proj):
        rows = slice(c * chunk, (c + 1) * chunk)
        o = 0
        qa = proj[:, o:o + NA_WIDTH]; o += NA_WIDTH
        ka = proj[:, o:o + NA_WIDTH]; o += NA_WIDTH
        va = proj[:, o:o + NA_WIDTH]; o += NA_WIDTH
        qb = proj[:, o:o + GQA_WIDTH]; o += GQA_WIDTH
        kb = proj[:, o:o + GQA_KV_WIDTH]; o += GQA_KV_WIDTH
        vb = proj[:, o:o + GQA_KV_WIDTH]
        qa_ref[rows, :] = (head_norm(qa, gq_a_ref[...]) * scale).astype(BF16)
        ka_ref[rows, :] = head_norm(ka, gk_a_ref[...]).astype(BF16)
        va_ref[rows, :] = va.astype(BF16)
        qb_ref[rows, :] = (rope(head_norm(qb, gq_b_ref[...]), rows) * scale).astype(BF16)
        kb_ref[rows, :] = rope(head_norm(kb, gk_b_ref[...]), rows).astype(BF16)
        vb_ref[rows, :] = vb.astype(BF16)

    _software_pipeline(x_ref.shape[0] // chunk, project, finish)


def _inproj(x2, g1, w_bf, bd, gq_a, gk_a, gq_b, gk_b, cos_t, sn_t, sp_t, seq, tm):
    n, d = x2.shape
    in_w = w_bf.shape[1]
    per_seq = seq // tm
    row = lambda i: (i, 0)
    const = lambda i: (0, 0)
    pos = lambda i: (i % per_seq, 0)
    outs = [jax.ShapeDtypeStruct((n, w), BF16)
            for w in (NA_WIDTH, NA_WIDTH, NA_WIDTH, GQA_WIDTH, GQA_KV_WIDTH, GQA_KV_WIDTH)]
    return pl.pallas_call(
        functools.partial(_inproj_kernel, chunk=256),
        grid=(n // tm,),
        in_specs=[
            pl.BlockSpec((tm, d), row),
            pl.BlockSpec((1, d), const),
            pl.BlockSpec((d, in_w), const),
            pl.BlockSpec(bd.shape, const),
            pl.BlockSpec((1, NA_WIDTH), const),
            pl.BlockSpec((1, NA_WIDTH), const),
            pl.BlockSpec((1, GQA_WIDTH), const),
            pl.BlockSpec((1, GQA_KV_WIDTH), const),
            pl.BlockSpec((tm, LANES), pos),
            pl.BlockSpec((tm, LANES), pos),
            pl.BlockSpec((tm, LANES), pos),
        ],
        out_specs=[pl.BlockSpec((tm, s.shape[1]), row) for s in outs],
        out_shape=outs,
        compiler_params=_cparams(("parallel",)),
        name="inproj",
    )(x2, g1, w_bf, bd, gq_a, gk_a, gq_b, gk_b, cos_t, sn_t, sp_t)


def _na_kernel(q_ref, k_ref, v_ref, bias_ref, g_ref, o_ref, *, rows, bounded_scores):
    nq = NA_QROWS * GRID_W
    nk = NA_KROWS * GRID_W
    nblk = rows // NA_QROWS
    nch = NA_WIDTH // LANES
    low = lax.broadcasted_iota(jnp.int32, (nq, LANES), 1) < HEAD_DIM
    top = lax.broadcasted_iota(jnp.int32, (LANES, nq), 0) < HEAD_DIM

    def block(jj):
        j = NA_STEP_BLOCKS * pl.program_id(1) + jj
        krow0 = jnp.clip(NA_QROWS * j - NA_WIN_H // 2, 0, rows - NA_KROWS)
        k0 = pl.multiple_of(krow0 * GRID_W, GRID_W)
        pat = jnp.clip(j, 0, 2) + jnp.clip(j - (nblk - 3), 0, 2)
        return k0, pat

    blocks = [block(jj) for jj in range(NA_STEP_BLOCKS)]

    def scores(s):
        jj, c = divmod(s, nch)
        k0, pat = blocks[jj]
        cs = slice(c * LANES, (c + 1) * LANES)
        kc = k_ref[pl.ds(k0, nk), cs]
        qc = q_ref[jj * nq:(jj + 1) * nq, cs]
        zero = jnp.zeros_like(qc)
        qbd = jnp.concatenate([jnp.where(low, qc, zero), jnp.where(low, zero, qc)], axis=0)
        return _dot_nt(kc, qbd) + bias_ref[pat, c]

    def attend(s, st):
        jj, c = divmod(s, nch)
        k0, _ = blocks[jj]
        if not bounded_scores:
            st = st - jnp.max(st, axis=0, keepdims=True)
        p = jnp.exp2(st)
        l = jnp.sum(p, axis=0, keepdims=True)
        vc = v_ref[pl.ds(k0, nk), c * LANES:(c + 1) * LANES]
        ot = _dot_tn(vc, p.astype(BF16)) / l
        return jnp.where(top, ot[:, :nq], ot[:, nq:]).T

    outs = _software_pipeline(NA_STEP_BLOCKS * nch, scores, attend)
    for jj in range(NA_STEP_BLOCKS):
        o = jnp.concatenate(outs[jj * nch:(jj + 1) * nch], axis=1)
        o_ref[jj * nq:(jj + 1) * nq, :] = _rms(o, g_ref[...]).astype(BF16)


def _na_attn(qa, ka, va, bias, g, batch, seq, bounded_scores):
    rows = seq // GRID_W
    nq = NA_STEP_BLOCKS * NA_QROWS * GRID_W
    nblk = seq // nq
    return pl.pallas_call(
        functools.partial(_na_kernel, rows=rows, bounded_scores=bounded_scores),
        grid=(batch, nblk),
        in_specs=[
            pl.BlockSpec((nq, NA_WIDTH), lambda b, j: (b * nblk + j, 0)),
            pl.BlockSpec((seq, NA_WIDTH), lambda b, j: (b, 0)),
            pl.BlockSpec((seq, NA_WIDTH), lambda b, j: (b, 0)),
            pl.BlockSpec(bias.shape, lambda b, j: (0, 0, 0, 0)),
            pl.BlockSpec((1, NA_WIDTH), lambda b, j: (0, 0)),
        ],
        out_specs=pl.BlockSpec((nq, NA_WIDTH), lambda b, j: (b * nblk + j, 0)),
        out_shape=jax.ShapeDtypeStruct(qa.shape, BF16),
        compiler_params=_cparams(("parallel", "arbitrary")),
        name="na_attn",
    )(qa, ka, va, bias, g)


def _na_bias_table(rpb, rows):
    nblk = rows // NA_QROWS
    pats = [0, 1, 2, nblk - 2, nblk - 1]
    kh, kw = NA_WIN_H, NA_WIN_W
    nh, nrel_r, nrel_c = rpb.shape
    kc = np.arange(GRID_W)[:, None]
    qc = np.arange(GRID_W)[None, :]
    cs = np.clip(qc - kw // 2, 0, GRID_W - kw)
    col_ok = (kc >= cs) & (kc < cs + kw)
    ci = np.clip(kc - qc + (kw - 1), 0, nrel_c - 1)
    pick = (ci[None] == np.arange(nrel_c)[:, None, None]) & col_ok[None]
    toep = jnp.einsum('hdj,jkq->hdkq', rpb, jnp.asarray(pick, F32), precision=lax.Precision.HIGHEST)
    toep = jnp.where(col_ok[None, None], toep * LOG2E, MASK_NEG)
    masked = jnp.full((nh, GRID_W, GRID_W), MASK_NEG, F32)
    pat_blocks = []
    for j in pats:
        krow0 = int(np.clip(NA_QROWS * j - kh // 2, 0, rows - NA_KROWS))
        key_rows = []
        for kr in range(krow0, krow0 + NA_KROWS):
            per_q = []
            for qr in range(NA_QROWS * j, NA_QROWS * (j + 1)):
                rs = int(np.clip(qr - kh // 2, 0, rows - kh))
                per_q.append(toep[:, kr - qr + kh - 1] if rs <= kr < rs + kh else masked)
            key_rows.append(jnp.stack(per_q, axis=2))
        pat_blocks.append(jnp.stack(key_rows, axis=1))
    dense = jnp.stack(pat_blocks, axis=0)
    p = len(pats)
    nk, nq = NA_KROWS * GRID_W, NA_QROWS * GRID_W
    dense = dense.reshape(p, nh // 2, 2, nk, nq).transpose(0, 1, 3, 2, 4)
    return dense.reshape(p, nh // 2, nk, 2 * nq)


def _gqa_kernel(q_ref, k_ref, v_ref, g_ref, o_ref, *, tq, bounded_scores):
    k = k_ref[...]
    v = v_ref[...]
    kv_low = lax.broadcasted_iota(jnp.int32, v.shape, 1) < HEAD_DIM
    one = jnp.ones_like(v)
    v_aug = (jnp.where(kv_low, v, one), jnp.where(kv_low, one, v))
    low = lax.broadcasted_iota(jnp.int32, (tq, LANES), 1) < HEAD_DIM
    top = lax.broadcasted_iota(jnp.int32, (LANES, tq), 0) < HEAD_DIM
    nch = GQA_WIDTH // LANES

    def scores(i):
        c, grp = divmod(i, GQA_KV_HEADS)
        qc = q_ref[:, c * LANES:(c + 1) * LANES]
        keep = low if grp == 0 else jnp.logical_not(low)
        return _dot_nt(k, jnp.where(keep, qc, jnp.zeros_like(qc)))

    def attend(i, st):
        grp = i % GQA_KV_HEADS
        if not bounded_scores:
            st = st - jnp.max(st, axis=0, keepdims=True)
        p = jnp.exp2(st).astype(BF16)
        return _dot_tn(v_aug[grp], p)

    ots = _software_pipeline(nch * GQA_KV_HEADS, scores, attend, depth=2)
    outs = []
    for c in range(nch):
        o0, o1 = ots[GQA_KV_HEADS * c], ots[GQA_KV_HEADS * c + 1]
        l0 = o0[HEAD_DIM:HEAD_DIM + 1, :]
        l1 = o1[0:1, :]
        outs.append(jnp.where(top, o0 / l0, o1 / l1).T)
    o = jnp.concatenate(outs, axis=1)
    o_ref[...] = _rms(o, g_ref[...]).astype(BF16)


def _gqa_attn(qb, kb, vb, g, batch, seq, tq, bounded_scores):
    nblk = seq // tq
    return pl.pallas_call(
        functools.partial(_gqa_kernel, tq=tq, bounded_scores=bounded_scores),
        grid=(batch, nblk),
        in_specs=[
            pl.BlockSpec((tq, GQA_WIDTH), lambda b, i: (b * nblk + i, 0)),
            pl.BlockSpec((seq, GQA_KV_WIDTH), lambda b, i: (b, 0)),
            pl.BlockSpec((seq, GQA_KV_WIDTH), lambda b, i: (b, 0)),
            pl.BlockSpec((1, GQA_WIDTH), lambda b, i: (0, 0)),
        ],
        out_specs=pl.BlockSpec((tq, GQA_WIDTH), lambda b, i: (b * nblk + i, 0)),
        out_shape=jax.ShapeDtypeStruct(qb.shape, BF16),
        compiler_params=_cparams(("parallel", "arbitrary")),
        name="gqa_attn",
    )(qb, kb, vb, g)


def _post_kernel(x_ref, ma_ref, mb_ref, wo_ref, g2_ref, wr_ref, h_ref, hn_ref, aff_ref, *, n_experts, chunk):
    def project(c):
        rows = slice(c * chunk, (c + 1) * chunk)
        return _dot(ma_ref[rows, :], wo_ref[:NA_WIDTH, :]) + _dot(mb_ref[rows, :], wo_ref[NA_WIDTH:, :])

    def route(c, attn):
        rows = slice(c * chunk, (c + 1) * chunk)
        h = x_ref[rows, :] + attn
        h_ref[rows, :] = h
        hn = _rms(h, g2_ref[...]).astype(BF16)
        hn_ref[rows, :] = hn
        logits = _dot(hn, wr_ref[...])
        lane = lax.broadcasted_iota(jnp.int32, logits.shape, 1)
        logits = jnp.where(lane < n_experts, logits, MASK_NEG)
        m = jnp.max(logits, axis=-1, keepdims=True)
        e = jnp.exp(logits - m)
        aff_ref[rows, :] = e / jnp.sum(e, axis=-1, keepdims=True)

    _software_pipeline(x_ref.shape[0] // chunk, project, route)


def _post(x2, ma, mb, wo_bf, g2, wr_bf, n_experts, tm):
    n, d = x2.shape
    row = lambda i: (i, 0)
    const = lambda i: (0, 0)
    return pl.pallas_call(
        functools.partial(_post_kernel, n_experts=n_experts, chunk=256),
        grid=(n // tm,),
        in_specs=[
            pl.BlockSpec((tm, d), row),
            pl.BlockSpec((tm, NA_WIDTH), row),
            pl.BlockSpec((tm, GQA_WIDTH), row),
            pl.BlockSpec(wo_bf.shape, const),
            pl.BlockSpec((1, d), const),
            pl.BlockSpec(wr_bf.shape, const),
        ],
        out_specs=[pl.BlockSpec((tm, d), row), pl.BlockSpec((tm, d), row), pl.BlockSpec((tm, LANES), row)],
        out_shape=[jax.ShapeDtypeStruct((n, d), F32), jax.ShapeDtypeStruct((n, d), BF16),
                   jax.ShapeDtypeStruct((n, LANES), F32)],
        compiler_params=_cparams(("parallel",)),
        name="post",
    )(x2, ma, mb, wo_bf, g2, wr_bf)


def _route_kernel(aff_ref, hn_ref, x_ref, gate_ref, posT_ref, tri_ref, aT_ref, pos_ref, *, n_experts, cap):
    seq = aff_ref.shape[0]
    rb = 256

    @pl.when(pl.program_id(0) == 0)
    def _():
        col = lax.broadcasted_iota(jnp.int32, (rb, seq), 1)
        row = lax.broadcasted_iota(jnp.int32, (rb, seq), 0)

        def fill(i, carry):
            r0 = pl.multiple_of(i * rb, rb)
            tri_ref[pl.ds(r0, rb), :] = jnp.where(row + r0 < col, 1.0, 0.0).astype(BF16)
            return carry

        lax.fori_loop(0, seq // rb, fill, 0)

    aT = aff_ref[...].T[:n_experts, :]
    def reaches(cand):
        cnt = jnp.sum(jnp.where(aT >= pltpu.bitcast(cand, F32), 1.0, 0.0), axis=1, keepdims=True)
        return cnt >= cap

    def search(i, prefix):
        sh = 28 - 2 * i
        c1, c2, c3 = prefix | (1 << sh), prefix | (2 << sh), prefix | (3 << sh)
        return jnp.where(reaches(c3), c3, jnp.where(reaches(c2), c2, jnp.where(reaches(c1), c1, prefix)))

    top_bit = jnp.full((n_experts, 1), 1 << 30, jnp.int32)
    start = jnp.where(reaches(top_bit), top_bit, jnp.zeros_like(top_bit))
    thr_bits = lax.fori_loop(0, 15, search, start)
    thr = pltpu.bitcast(thr_bits, F32)
    gt = aT > thr
    eq = aT == thr
    n_gt = jnp.sum(jnp.where(gt, 1.0, 0.0), axis=1, keepdims=True)
    both = jnp.concatenate([jnp.where(eq, 1.0, 0.0), jnp.where(gt, 1.0, 0.0)], axis=0).astype(BF16)
    prefix = _dot(both, tri_ref[...])
    eq_rank, gt_rank = prefix[:n_experts], prefix[n_experts:]
    need = cap - n_gt
    sel = gt | (eq & (eq_rank < need))
    pos = gt_rank + jnp.minimum(eq_rank, need)
    posm = jnp.where(sel, pos, -1.0)
    pos_ref[...] = posm
    aT_ref[...] = aT
    pad = jnp.full((LANES - n_experts, seq), -1.0, F32)
    posT_ref[...] = jnp.concatenate([posm, pad], axis=0).T

    hn = hn_ref[...]
    slot = lax.broadcasted_iota(jnp.int32, (cap, seq), 0).astype(F32)

    def gather(e, carry):
        hit = pos_ref[pl.ds(e, 1), :] == slot
        onehot = jnp.where(hit, 1.0, 0.0).astype(BF16)
        x_ref[e] = _dot(onehot, hn).astype(BF16)
        gate_ref[e] = jnp.sum(jnp.where(hit, aT_ref[pl.ds(e, 1), :], 0.0), axis=1, keepdims=True)
        return carry

    lax.fori_loop(0, n_experts, gather, 0)


def _route(aff, hn, batch, seq, n_experts, cap):
    d = hn.shape[1]
    return pl.pallas_call(
        functools.partial(_route_kernel, n_experts=n_experts, cap=cap),
        grid=(batch,),
        in_specs=[
            pl.BlockSpec((seq, LANES), lambda b: (b, 0)),
            pl.BlockSpec((seq, d), lambda b: (b, 0)),
        ],
        out_specs=[
            pl.BlockSpec((n_experts, cap, d), lambda b: (0, b, 0)),
            pl.BlockSpec((n_experts, cap, 1), lambda b: (0, b, 0)),
            pl.BlockSpec((seq, LANES), lambda b: (b, 0)),
        ],
        out_shape=[
            jax.ShapeDtypeStruct((n_experts, batch * cap, d), BF16),
            jax.ShapeDtypeStruct((n_experts, batch * cap, 1), F32),
            jax.ShapeDtypeStruct((batch * seq, LANES), F32),
        ],
        scratch_shapes=[
            pltpu.VMEM((seq, seq), BF16),
            pltpu.VMEM((n_experts, seq), F32),
            pltpu.VMEM((n_experts, seq), F32),
        ],
        compiler_params=_cparams(("arbitrary",)),
        name="route",
    )(aff, hn)


def _ffn_kernel(x_ref, gate_ref, wg_ref, wu_ref, wd_ref, y_ref, acc_ref, *, chunk):
    f = pl.program_id(2)
    wg = wg_ref[...].astype(BF16)
    wu = wu_ref[...].astype(BF16)
    wd = wd_ref[...].astype(BF16)

    @pl.when(f == 0)
    def _():
        acc_ref[...] = jnp.zeros_like(acc_ref)

    def hidden(c):
        x = x_ref[c * chunk:(c + 1) * chunk, :]
        g = _dot(x, wg)
        u = _dot(x, wu)
        return (g * (1.0 / (1.0 + jnp.exp(-g))) * u).astype(BF16)

    def project(c, a):
        acc_ref[c * chunk:(c + 1) * chunk, :] += _dot(a, wd)

    _software_pipeline(x_ref.shape[0] // chunk, hidden, project)

    @pl.when(f == pl.num_programs(2) - 1)
    def _():
        y_ref[...] = (acc_ref[...] * gate_ref[...]).astype(BF16)


def _ffn(xg, gate, w_gate, w_up, w_down, tm, tf):
    e, m, d = xg.shape
    dff = w_gate.shape[2]
    return pl.pallas_call(
        functools.partial(_ffn_kernel, chunk=512),
        grid=(e, m // tm, dff // tf),
        in_specs=[
            pl.BlockSpec((None, tm, d), lambda e, i, f: (e, i, 0)),
            pl.BlockSpec((None, tm, 1), lambda e, i, f: (e, i, 0)),
            pl.BlockSpec((None, d, tf), lambda e, i, f: (e, 0, f)),
            pl.BlockSpec((None, d, tf), lambda e, i, f: (e, 0, f)),
            pl.BlockSpec((None, tf, d), lambda e, i, f: (e, f, 0)),
        ],
        out_specs=pl.BlockSpec((None, tm, d), lambda e, i, f: (e, i, 0)),
        out_shape=jax.ShapeDtypeStruct((e, m, d), BF16),
        scratch_shapes=[pltpu.VMEM((tm, d), F32)],
        compiler_params=_cparams(("parallel", "parallel", "arbitrary")),
        name="ffn",
    )(xg, gate, w_gate, w_up, w_down)


def _combine_kernel(h_ref, posT_ref, y_ref, o_ref, *, n_experts, cap):
    tt = h_ref.shape[0]
    post = posT_ref[...]
    slot = lax.broadcasted_iota(jnp.int32, (tt, cap), 1).astype(F32)
    parts = []
    for e in range(n_experts):
        hit = jnp.broadcast_to(post[:, e:e + 1], (tt, cap)) == slot
        parts.append(jnp.where(hit, 1.0, 0.0).astype(BF16))
    onehot = jnp.concatenate(parts, axis=1)
    d = y_ref.shape[2]
    y = y_ref[...].reshape(n_experts * cap, d)
    o_ref[...] = h_ref[...] + _dot(onehot, y)


def _combine(h, posT, y, batch, seq, n_experts, cap, tt):
    n, d = h.shape
    per_seq = seq // tt
    return pl.pallas_call(
        functools.partial(_combine_kernel, n_experts=n_experts, cap=cap),
        grid=(batch, per_seq),
        in_specs=[
            pl.BlockSpec((tt, d), lambda b, i: (b * per_seq + i, 0)),
            pl.BlockSpec((tt, LANES), lambda b, i: (b * per_seq + i, 0)),
            pl.BlockSpec((n_experts, cap, d), lambda b, i: (0, b, 0)),
        ],
        out_specs=pl.BlockSpec((tt, d), lambda b, i: (b * per_seq + i, 0)),
        out_shape=jax.ShapeDtypeStruct((n, d), F32),
        compiler_params=_cparams(("parallel", "arbitrary")),
        name="combine",
    )(h, posT, y)


def _rope_tables(seq):
    t = np.arange(seq)
    row = (t // GRID_W).astype(np.float32)
    col = (t % GRID_W).astype(np.float32)
    half = HEAD_DIM // 2
    inv_freq = 1.0 / (ROPE_THETA ** (jnp.arange(0, half, 2, dtype=F32) / half))
    ang = jnp.concatenate([row[:, None] * inv_freq[None], col[:, None] * inv_freq[None]], axis=-1)
    cos, sin = jnp.cos(ang), jnp.sin(ang)
    lane = np.arange(LANES)
    pair = (lane % HEAD_DIM) // 2
    even = (lane % 2 == 0)[None, :]
    cos_l = cos[:, pair]
    sin_l = sin[:, pair]
    return cos_l, jnp.where(even, -sin_l, 0.0), jnp.where(even, 0.0, sin_l)


def _layer(x2, batch, seq, norm1_g, w_in, na_q_g, na_k_g, na_rpb, gqa_q_g, gqa_k_g,
           na_out_g, gqa_out_g, w_out, norm2_g, w_router, w_gate, w_up, w_down):
    d = x2.shape[1]
    n_experts = w_router.shape[1]
    cap = EC_CAPACITY_FACTOR * seq // n_experts
    rows = seq // GRID_W

    per_kv = GQA_HEADS // GQA_KV_HEADS
    head_order = np.arange(GQA_HEADS).reshape(GQA_KV_HEADS, per_kv).T.reshape(-1)
    perm = (head_order[:, None] * HEAD_DIM + np.arange(HEAD_DIM)[None, :]).reshape(-1)
    qb0 = 3 * NA_WIDTH
    w_in_p = jnp.concatenate([w_in[:, :qb0], w_in[:, qb0 + perm], w_in[:, qb0 + GQA_WIDTH:]], axis=1)
    w_out_p = jnp.concatenate([w_out[:NA_WIDTH], w_out[NA_WIDTH + perm]], axis=0)
    gqa_out_g_p = gqa_out_g[perm]

    hd = np.arange(MXU_TILE) // HEAD_DIM
    bd = jnp.asarray(hd[:, None] == hd[None, :], dtype=BF16)
    cos_t, sn_t, sp_t = _rope_tables(seq)
    tile_g = lambda g, heads: jnp.tile(g, heads)[None, :]

    qa, ka, va, qb, kb, vb = _inproj(
        x2, norm1_g[None, :], w_in_p.astype(BF16), bd,
        tile_g(na_q_g, NA_HEADS), tile_g(na_k_g, NA_HEADS),
        tile_g(gqa_q_g, GQA_HEADS), tile_g(gqa_k_g, GQA_KV_HEADS),
        cos_t, sn_t, sp_t, seq, tm=1024)

    bias = _na_bias_table(na_rpb, rows)
    qk_bound = lambda gq, gk: 1.02 * HEAD_DIM ** 0.5 * LOG2E * jnp.max(jnp.abs(gq)) * jnp.max(jnp.abs(gk))
    na_bound = qk_bound(na_q_g, na_k_g) + LOG2E * jnp.max(jnp.abs(na_rpb))
    na = lambda bounded: functools.partial(_na_attn, batch=batch, seq=seq, bounded_scores=bounded)
    ma = lax.cond(na_bound <= SCORE_LIMIT, na(True), na(False), qa, ka, va, bias, na_out_g[None, :])
    bound = qk_bound(gqa_q_g, gqa_k_g)
    gqa = lambda bounded: functools.partial(_gqa_attn, batch=batch, seq=seq, tq=512, bounded_scores=bounded)
    mb = lax.cond(bound <= SCORE_LIMIT, gqa(True), gqa(False), qb, kb, vb, gqa_out_g_p[None, :])

    wr = jnp.zeros((d, LANES), F32).at[:, :n_experts].set(w_router).astype(BF16)
    h, hn, aff = _post(x2, ma, mb, w_out_p.astype(BF16), norm2_g[None, :], wr, n_experts, tm=1024)

    xg, gate, posT = _route(aff, hn, batch, seq, n_experts, cap)
    y = _ffn(xg, gate, w_gate, w_up, w_down, tm=2048, tf=256)
    return _combine(h, posT, y, batch, seq, n_experts, cap, tt=512)


def kernel(x, norm1_g, w_in, na_q_norm_g, na_k_norm_g, na_rpb, gqa_q_norm_g, gqa_k_norm_g,
           na_out_g, gqa_out_g, w_out, norm2_g, w_router, w_gate, w_up, w_down):
    batch, seq, d = x.shape
    x2 = x.reshape(batch * seq, d)
    for l in range(norm1_g.shape[0]):
        x2 = _layer(x2, batch, seq, norm1_g[l], w_in[l], na_q_norm_g[l], na_k_norm_g[l], na_rpb[l],
                    gqa_q_norm_g[l], gqa_k_norm_g[l], na_out_g[l], gqa_out_g[l], w_out[l],
                    norm2_g[l], w_router[l], w_gate[l], w_up[l], w_down[l])
    return x2.reshape(batch, seq, d)
```
